```python
import math
import jax, jax.numpy as jnp
from jax import lax
import numpy as np

D_MODEL = 1024
BATCH = 4
SEQ = 8192
DEPTH = 4

CHUNK = 64
N_MIXERS = 3
EPS = 1e-6

DSA_HEADS = 16
DSA_HEAD_DIM = D_MODEL // DSA_HEADS
IDX_HEADS = 8
IDX_DIM = 64
TOPK_MAX = 256
Q_BLOCK = CHUNK
DSA_IN = 3 * D_MODEL + IDX_HEADS * IDX_DIM + IDX_DIM + IDX_HEADS
NEG_INF = -1e30

RWKV_HEAD = 64
RWKV_HEADS = D_MODEL // RWKV_HEAD
LORA_DECAY = 64
LORA_A = 64
LORA_GATE = 128
GN_EPS = 64e-5

S5_GROUP = 16
S5_GROUPS = D_MODEL // S5_GROUP
S5_STATE = 64
DT_MIN = 1e-3
DT_MAX = 1e-1

D_FF = 2816
CONV_W = 3

N_A = (DEPTH + 2) // 3
N_B = (DEPTH + 1) // 3
N_C = DEPTH // 3

kernel_name = 'hybrid_dsa_rwkv7_s5_convffn'


def rms_norm(x, g):
    x32 = x.astype(jnp.float32)
    y = x32 * lax.rsqrt(jnp.mean(x32 * x32, axis=-1, keepdims=True) + EPS)
    return (y * g.astype(jnp.float32)).astype(x.dtype)


def dsa_mixer(h, w_in, q_gain, k_gain, kidx_gain, w_out):
    bsz, seq, _ = h.shape
    topk = min(TOPK_MAX, seq // 4)
    D = D_MODEL
    proj = h @ w_in
    q, k, v, q_idx, k_idx, w_idx = jnp.split(
        proj, [D, 2 * D, 3 * D, 3 * D + IDX_HEADS * IDX_DIM, 3 * D + IDX_HEADS * IDX_DIM + IDX_DIM], axis=-1)
    q = rms_norm(q.reshape(bsz, seq, DSA_HEADS, DSA_HEAD_DIM), q_gain)
    k = rms_norm(k.reshape(bsz, seq, DSA_HEADS, DSA_HEAD_DIM), k_gain)
    v = v.reshape(bsz, seq, DSA_HEADS, DSA_HEAD_DIM)
    q_idx = q_idx.reshape(bsz, seq, IDX_HEADS, IDX_DIM)
    k_idx32 = rms_norm(k_idx, kidx_gain).astype(jnp.float32)
    w_idx = w_idx * (IDX_HEADS ** -0.5)

    nblk = seq // Q_BLOCK
    key_pos = jnp.arange(seq)

    def to_blocks(t):
        return jnp.moveaxis(t.reshape(bsz, nblk, Q_BLOCK, *t.shape[2:]), 1, 0)

    def block(args):
        blk, qb, qib, wb = args
        limit = (blk + 1) * Q_BLOCK
        logits = jnp.einsum('bqhd,bsd->bqhs', qib.astype(jnp.float32), k_idx32) * (IDX_DIM ** -0.5)
        score = jnp.einsum('bqh,bqhs->bqs', wb.astype(jnp.float32), jax.nn.relu(logits))
        score = jnp.where(key_pos[None, None, :] < limit, score, NEG_INF)
        _, idx = lax.top_k(score, topk)
        k_sel = jax.vmap(lambda kk, ii: kk[ii])(k, idx)
        v_sel = jax.vmap(lambda vv, ii: vv[ii])(v, idx)
        att = jnp.einsum('bqhd,bqkhd->bqhk', qb.astype(jnp.float32), k_sel.astype(jnp.float32))
        att = att * (DSA_HEAD_DIM ** -0.5)
        att = jnp.where((idx < limit)[:, :, None, :], att, NEG_INF)
        p = jax.nn.softmax(att, axis=-1)
        o = jnp.einsum('bqhk,bqkhd->bqhd', p, v_sel.astype(jnp.float32))
        return o.astype(qb.dtype)

    out = lax.map(block, (jnp.arange(nblk), to_blocks(q), to_blocks(q_idx), to_blocks(w_idx)))
    out = jnp.moveaxis(out, 0, 1).reshape(bsz, seq, D)
    return out @ w_out


def rwkv7_mixer(h, mu, w_rkv, w0, w1, w2, a0, a1, a2, g1, g2, k_k, k_a, r_k, ln_g, ln_b, w_out):
    bsz, seq, _ = h.shape
    f32 = jnp.float32
    h_prev = jnp.pad(h, ((0, 0), (1, 0), (0, 0)))[:, :-1]
    dh = h_prev - h
    xs = h[None] + dh[None] * mu[:, None, None, :]
    rkv = jnp.einsum('nbsd,nde->nbse', xs[:3], w_rkv)
    r, k, v = rkv[0], rkv[1], rkv[2]
    xw, xa, xg = xs[3], xs[4], xs[5]
    w_log = -jax.nn.softplus(-(w0 + jnp.tanh(xw @ w1) @ w2)) - 0.5
    decay = jnp.exp(-jnp.exp(w_log.astype(f32)))
    a = jax.nn.sigmoid(a0 + (xa @ a1) @ a2)
    g = jax.nn.sigmoid(xg @ g1) @ g2

    def heads(t):
        return t.reshape(bsz, seq, RWKV_HEADS, RWKV_HEAD).astype(f32)

    kk = heads(k * k_k)
    kk = kk / jnp.maximum(jnp.sqrt(jnp.sum(kk * kk, axis=-1, keepdims=True)), 1e-12)
    k = k * (1.0 + (a - 1.0) * k_a)
    r_h, k_h, v_h, a_h, w_h = heads(r), heads(k), heads(v), heads(a), heads(decay)

    def tm(t):
        return jnp.moveaxis(t, 1, 0)

    def step(state, inp):
        r_t, w_t, k_t, v_t, kk_t, a_t = inp
        sa = jnp.einsum('bhij,bhj->bhi', state, -kk_t)
        state = (state * w_t[:, :, None, :]
                 + sa[..., None] * (kk_t * a_t)[:, :, None, :]
                 + v_t[..., None] * k_t[:, :, None, :])
        y = jnp.einsum('bhij,bhj->bhi', state, r_t)
        return state, y

    state0 = jnp.zeros((bsz, RWKV_HEADS, RWKV_HEAD, RWKV_HEAD), f32)
    _, y = lax.scan(step, state0, (tm(r_h), tm(w_h), tm(k_h), tm(v_h), tm(kk), tm(a_h)))
    y = jnp.moveaxis(y, 0, 1)
    mean = jnp.mean(y, axis=-1, keepdims=True)
    var = jnp.mean(jnp.square(y - mean), axis=-1, keepdims=True)
    y = (y - mean) * lax.rsqrt(var + GN_EPS)
    y = y * ln_g.astype(f32).reshape(RWKV_HEADS, RWKV_HEAD) + ln_b.astype(f32).reshape(RWKV_HEADS, RWKV_HEAD)
    bonus = jnp.sum(r_h * k_h * r_k.astype(f32), axis=-1, keepdims=True) * v_h
    y = (y + bonus).reshape(bsz, seq, D_MODEL).astype(h.dtype) * g
    return y @ w_out


def s5_mixer(h, a_re, a_im, log_step, b_re, b_im, c_re, c_im, d_skip, w_glu, b_glu):
    bsz, seq, _ = h.shape
    f32 = jnp.float32
    step = jnp.exp(log_step.astype(f32))[:, None]
    lam = lax.complex(a_re.astype(f32), a_im.astype(f32))
    lam_bar = jnp.exp(lam * step)
    b = lax.complex(b_re.astype(f32), b_im.astype(f32))
    b_bar = ((lam_bar - 1.0) / lam)[..., None] * b
    c = lax.complex(c_re.astype(f32), c_im.astype(f32))
    u = jnp.swapaxes(h, 0, 1).astype(f32).reshape(seq, bsz, S5_GROUPS, S5_GROUP)
    bu = jnp.einsum('gpc,sbgc->sbgp', b_bar, u.astype(jnp.complex64))
    lam_t = jnp.broadcast_to(lam_bar, (seq, 1, S5_GROUPS, S5_STATE))

    def combine(e_i, e_j):
        a_i, x_i = e_i
        a_j, x_j = e_j
        return a_j * a_i, a_j * x_i + x_j

    _, states = lax.associative_scan(combine, (lam_t, bu), axis=0)
    y = jnp.real(jnp.einsum('gcp,sbgp->sbgc', c, states)).reshape(seq, bsz, D_MODEL)
    y = jnp.swapaxes(y, 0, 1) + d_skip.astype(f32) * h.astype(f32)
    y = jax.nn.gelu(y).astype(h.dtype)
    z = y @ w_glu + b_glu
    z1, z2 = jnp.split(z, 2, axis=-1)
    return z1 * jax.nn.sigmoid(z2)


def conv_ffn(h, w_up, conv_w, conv_b, w_down):
    u = h @ w_up
    u = lax.conv_general_dilated(
        u, conv_w[:, None, :], window_strides=(1,), padding=[(CONV_W - 1, 0)],
        dimension_numbers=('NWC', 'WIO', 'NWC'), feature_group_count=u.shape[-1]) + conv_b
    gate, val = jnp.split(u, 2, axis=-1)
    return (jax.nn.silu(gate) * val) @ w_down


def setup_inputs(seed: int = 0) -> dict:
    key = jax.random.key(seed)
    ks = iter(jax.random.split(key, 64))
    D, F = D_MODEL, D_FF
    G, P, GC = S5_GROUPS, S5_STATE, S5_GROUP

    def nrm(shape, scale):
        return jax.random.normal(next(ks), shape, jnp.float32) * scale

    def gain(shape):
        return 1.0 + nrm(shape, 0.02)

    x = nrm((BATCH, SEQ, D), 1.0)
    norm_mix = gain((DEPTH, D))
    norm_ffn = gain((DEPTH, D))
    dsa_w_in = nrm((N_A, D, DSA_IN), D ** -0.5)
    dsa_q_norm = gain((N_A, DSA_HEAD_DIM))
    dsa_k_norm = gain((N_A, DSA_HEAD_DIM))
    dsa_kidx_norm = gain((N_A, IDX_DIM))
    dsa_w_out = nrm((N_A, D, D), D ** -0.5)
    rwkv_mu = jax.random.uniform(next(ks), (N_B, 6, D), jnp.float32)
    rwkv_w_rkv = nrm((N_B, 3, D, D), D ** -0.5)
    rwkv_w0 = jax.random.uniform(next(ks), (N_B, D), jnp.float32, -5.0, 0.0)
    rwkv_w1 = nrm((N_B, D, LORA_DECAY), D ** -0.5)
    rwkv_w2 = nrm((N_B, LORA_DECAY, D), 0.1 * LORA_DECAY ** -0.5)
    rwkv_a0 = nrm((N_B, D), 0.1)
    rwkv_a1 = nrm((N_B, D, LORA_A), D ** -0.5)
    rwkv_a2 = nrm((N_B, LORA_A, D), 0.1 * LORA_A ** -0.5)
    rwkv_g1 = nrm((N_B, D, LORA_GATE), D ** -0.5)
    rwkv_g2 = nrm((N_B, LORA_GATE, D), LORA_GATE ** -0.5)
    rwkv_k_k = 0.85 + nrm((N_B, D), 0.02)
    rwkv_k_a = gain((N_B, D))
    rwkv_r_k = nrm((N_B, RWKV_HEADS, RWKV_HEAD), 0.1)
    rwkv_ln_g = gain((N_B, D))
    rwkv_ln_b = nrm((N_B, D), 0.01)
    rwkv_w_out = nrm((N_B, D, D), D ** -0.5)
    s5_a_re = -0.5 + nrm((N_C, G, P), 0.01)
    s5_a_im = jnp.broadcast_to(math.pi * jnp.arange(P, dtype=jnp.float32), (N_C, G, P))
    s5_log_step = jax.random.uniform(next(ks), (N_C, G), jnp.float32, math.log(DT_MIN), math.log(DT_MAX))
    s5_b_re = nrm((N_C, G, P, GC), (2 * GC) ** -0.5)
    s5_b_im = nrm((N_C, G, P, GC), (2 * GC) ** -0.5)
    s5_c_re = nrm((N_C, G, GC, P), P ** -0.5)
    s5_c_im = nrm((N_C, G, GC, P), P ** -0.5)
    s5_d = nrm((N_C, D), 1.0)
    s5_w_glu = nrm((N_C, D, 2 * D), D ** -0.5)
    s5_b_glu = nrm((N_C, 2 * D), 0.01)
    ffn_w_up = nrm((DEPTH, D, 2 * F), D ** -0.5)
    ffn_conv_w = nrm((DEPTH, CONV_W, 2 * F), CONV_W ** -0.5)
    ffn_conv_b = nrm((DEPTH, 2 * F), 0.01)
    ffn_w_down = nrm((DEPTH, F, D), F ** -0.5)
    return {
        'x': x, 'norm_mix': norm_mix, 'norm_ffn': norm_ffn,
        'dsa_w_in': dsa_w_in, 'dsa_q_norm': dsa_q_norm, 'dsa_k_norm': dsa_k_norm,
        'dsa_kidx_norm': dsa_kidx_norm, 'dsa_w_out': dsa_w_out,
        'rwkv_mu': rwkv_mu, 'rwkv_w_rkv': rwkv_w_rkv, 'rwkv_w0': rwkv_w0, 'rwkv_w1': rwkv_w1,
        'rwkv_w2': rwkv_w2, 'rwkv_a0': rwkv_a0, 'rwkv_a1': rwkv_a1, 'rwkv_a2': rwkv_a2,
        'rwkv_g1': rwkv_g1, 'rwkv_g2': rwkv_g2, 'rwkv_k_k': rwkv_k_k, 'rwkv_k_a': rwkv_k_a,
        'rwkv_r_k': rwkv_r_k, 'rwkv_ln_g': rwkv_ln_g, 'rwkv_ln_b': rwkv_ln_b, 'rwkv_w_out': rwkv_w_out,
        's5_a_re': s5_a_re, 's5_a_im': s5_a_im, 's5_log_step': s5_log_step,
        's5_b_re': s5_b_re, 's5_b_im': s5_b_im, 's5_c_re': s5_c_re, 's5_c_im': s5_c_im,
        's5_d': s5_d, 's5_w_glu': s5_w_glu, 's5_b_glu': s5_b_glu,
        'ffn_w_up': ffn_w_up, 'ffn_conv_w': ffn_conv_w, 'ffn_conv_b': ffn_conv_b, 'ffn_w_down': ffn_w_down,
    }


def reference(x, norm_mix, norm_ffn,
              dsa_w_in, dsa_q_norm, dsa_k_norm, dsa_kidx_norm, dsa_w_out,
              rwkv_mu, rwkv_w_rkv, rwkv_w0, rwkv_w1, rwkv_w2, rwkv_a0, rwkv_a1, rwkv_a2,
              rwkv_g1, rwkv_g2, rwkv_k_k, rwkv_k_a, rwkv_r_k, rwkv_ln_g, rwkv_ln_b, rwkv_w_out,
              s5_a_re, s5_a_im, s5_log_step, s5_b_re, s5_b_im, s5_c_re, s5_c_im,
              s5_d, s5_w_glu, s5_b_glu,
              ffn_w_up, ffn_conv_w, ffn_conv_b, ffn_w_down):
    for i in range(DEPTH):
        kind = i % N_MIXERS
        j = i // N_MIXERS
        h = rms_norm(x, norm_mix[i])
        if kind == 0:
            y = dsa_mixer(h, dsa_w_in[j], dsa_q_norm[j], dsa_k_norm[j], dsa_kidx_norm[j], dsa_w_out[j])
        elif kind == 1:
            y = rwkv7_mixer(h, rwkv_mu[j], rwkv_w_rkv[j], rwkv_w0[j], rwkv_w1[j], rwkv_w2[j],
                            rwkv_a0[j], rwkv_a1[j], rwkv_a2[j], rwkv_g1[j], rwkv_g2[j],
                            rwkv_k_k[j], rwkv_k_a[j], rwkv_r_k[j], rwkv_ln_g[j], rwkv_ln_b[j],
                            rwkv_w_out[j])
        else:
            y = s5_mixer(h, s5_a_re[j], s5_a_im[j], s5_log_step[j], s5_b_re[j], s5_b_im[j],
                         s5_c_re[j], s5_c_im[j], s5_d[j], s5_w_glu[j], s5_b_glu[j])
        x = x + y
        h = rms_norm(x, norm_ffn[i])
        x = x + conv_ffn(h, ffn_w_up[i], ffn_conv_w[i], ffn_conv_b[i], ffn_w_down[i])
    return x
```

```python
import functools
import math

import jax
import jax.numpy as jnp
from jax import lax
from jax.experimental import pallas as pl
from jax.experimental.pallas import tpu as pltpu

F32 = jnp.float32
BF16 = jnp.bfloat16
I32 = jnp.int32

EPS = 1e-6
NEG_INF = -1e30
LANES = 128
HEAD = 64
CHUNK = 64
TOPK_MAX = 256
GN_EPS = 64e-5
INT_MIN = -(2 ** 31)

VMEM_LIMIT = 56 * 1024 * 1024


def _cp(*sem):
    return pltpu.CompilerParams(dimension_semantics=sem, vmem_limit_bytes=VMEM_LIMIT)


def _nt(a, b):
    return lax.dot_general(a, b, (((1,), (1,)), ((), ())), preferred_element_type=F32)


def _tn(a, b):
    return lax.dot_general(a, b, (((0,), (0,)), ((), ())), preferred_element_type=F32)


def _dot(a, b):
    return jnp.dot(a, b, preferred_element_type=F32)


def _rms(x, g):
    ms = jnp.mean(x * x, axis=-1, keepdims=True)
    return x * lax.rsqrt(ms + EPS) * g


def _seg64_sum(x):
    lane = lax.broadcasted_iota(I32, x.shape, 1)
    for s in (32, 16, 8, 4, 2, 1):
        up = pltpu.roll(x, LANES - s, 1)
        dn = pltpu.roll(x, s, 1)
        x = x + jnp.where((lane & s) == 0, up, dn)
    return x


def _norm_kernel(x_ref, g_ref, o_ref):
    o_ref[...] = _rms(x_ref[...], g_ref[...]).astype(o_ref.dtype)


def rmsnorm(x2, g, out_dtype, tm=1024):
    t, d = x2.shape
    return pl.pallas_call(
        _norm_kernel,
        out_shape=jax.ShapeDtypeStruct((t, d), out_dtype),
        grid=(t // tm,),
        in_specs=[pl.BlockSpec((tm, d), lambda i: (i, 0)),
                  pl.BlockSpec((1, d), lambda i: (0, 0))],
        out_specs=pl.BlockSpec((tm, d), lambda i: (i, 0)),
        compiler_params=_cp("parallel"),
    )(x2, g.reshape(1, d))


def _mm_res_norm_kernel(a_ref, w_ref, x_ref, g_ref, xo_ref, ho_ref):
    xn = x_ref[...] + _dot(a_ref[...], w_ref[...])
    xo_ref[...] = xn
    ho_ref[...] = _rms(xn, g_ref[...]).astype(ho_ref.dtype)


def _mm_res_kernel(a_ref, w_ref, x_ref, xo_ref):
    xo_ref[...] = x_ref[...] + _dot(a_ref[...], w_ref[...])


def mm_res_norm(a, w, x2, g, h_dtype, tm=512):
    t, k = a.shape
    d = w.shape[1]
    row = lambda i: (i, 0)
    fixed = lambda i: (0, 0)
    in_specs = [pl.BlockSpec((tm, k), row), pl.BlockSpec((k, d), fixed), pl.BlockSpec((tm, d), row)]
    if g is None:
        return pl.pallas_call(
            _mm_res_kernel,
            out_shape=jax.ShapeDtypeStruct((t, d), F32),
            grid=(t // tm,), in_specs=in_specs, out_specs=pl.BlockSpec((tm, d), row),
            compiler_params=_cp("parallel"),
        )(a, w, x2), None
    return pl.pallas_call(
        _mm_res_norm_kernel,
        out_shape=(jax.ShapeDtypeStruct((t, d), F32), jax.ShapeDtypeStruct((t, d), h_dtype)),
        grid=(t // tm,),
        in_specs=in_specs + [pl.BlockSpec((1, d), fixed)],
        out_specs=(pl.BlockSpec((tm, d), row), pl.BlockSpec((tm, d), row)),
        compiler_params=_cp("parallel"),
    )(a, w, x2, g.reshape(1, d))


PREV_ROWS = 16


def _ffn_up_kernel(h_ref, hp_ref, wg_ref, wv_ref, cwg_ref, cwv_ref, cbg_ref, cbv_ref, o_ref, *, tiles_per_seq):
    i = pl.program_id(1)
    h = h_ref[...]
    hp = hp_ref[...]
    seq_start = (i % tiles_per_seq) == 0

    def branch(w_ref, cw_ref, cb_ref):
        w = w_ref[...]
        u = _dot(h, w)
        up = jnp.where(seq_start, 0.0, _dot(hp, w))
        rows = lax.broadcasted_iota(I32, u.shape, 0)
        last = up[PREV_ROWS - 1:PREV_ROWS, :]
        u1 = jnp.where(rows == 0, last, pltpu.roll(u, 1, 0))
        u2 = jnp.where(rows == 0, up[PREV_ROWS - 2:PREV_ROWS - 1, :],
                       jnp.where(rows == 1, last, pltpu.roll(u, 2, 0)))
        cw = cw_ref[...]
        return cw[0:1, :] * u2 + cw[1:2, :] * u1 + cw[2:3, :] * u + cb_ref[...]

    gate = branch(wg_ref, cwg_ref, cbg_ref)
    val = branch(wv_ref, cwv_ref, cbv_ref)
    o_ref[...] = (gate * jax.nn.sigmoid(gate) * val).astype(o_ref.dtype)


def ffn_up(h, w_up, conv_w, conv_b, seq, tm=256):
    t, d = h.shape
    f = w_up.shape[1] // 2
    tn = f // 2
    assert tn % LANES == 0 and seq % tm == 0
    nj = f // tn
    cw = jnp.zeros((8, 2 * f), F32).at[:conv_w.shape[0]].set(conv_w)
    cb = conv_b.reshape(1, 2 * f)
    prev = lambda j, i: (jnp.maximum(i * (tm // PREV_ROWS) - 1, 0), 0)
    return pl.pallas_call(
        functools.partial(_ffn_up_kernel, tiles_per_seq=seq // tm),
        out_shape=jax.ShapeDtypeStruct((t, f), BF16),
        grid=(nj, t // tm),
        in_specs=[pl.BlockSpec((tm, d), lambda j, i: (i, 0)),
                  pl.BlockSpec((PREV_ROWS, d), prev),
                  pl.BlockSpec((d, tn), lambda j, i: (0, j)),
                  pl.BlockSpec((d, tn), lambda j, i: (0, j + nj)),
                  pl.BlockSpec((8, tn), lambda j, i: (0, j)),
                  pl.BlockSpec((8, tn), lambda j, i: (0, j + nj)),
                  pl.BlockSpec((1, tn), lambda j, i: (0, j)),
                  pl.BlockSpec((1, tn), lambda j, i: (0, j + nj))],
        out_specs=pl.BlockSpec((tm, tn), lambda j, i: (i, j)),
        compiler_params=_cp("parallel", "arbitrary"),
    )(h, h, w_up, w_up, cw, cw, cb, cb)


PAIRS = 8
IDX_PAIRS = 4
TQ = 128
TK = 512


def _dsa_proj_kernel(h_ref, wq_ref, wk_ref, wvt_ref, wi_ref, gq_ref, gk_ref, gki_ref,
                     q_ref, k_ref, vt_ref, qi_ref, ki_ref, wi_out_ref):
    h = h_ref[...]
    tm = h.shape[0]

    def head_norm(y, g_ref, o_ref):
        for p in range(PAIRS):
            yp = y[:, p * LANES:(p + 1) * LANES]
            ms = _seg64_sum(yp * yp) * (1.0 / HEAD)
            o_ref[0, p] = (yp * lax.rsqrt(ms + EPS) * g_ref[:, p * LANES:(p + 1) * LANES]).astype(o_ref.dtype)

    head_norm(_dot(h, wq_ref[...]), gq_ref, q_ref)
    head_norm(_dot(h, wk_ref[...]), gk_ref, k_ref)
    vt = _nt(wvt_ref[...], h)
    vt_ref[0] = vt.reshape(PAIRS, LANES, tm).astype(vt_ref.dtype)
    idx = _dot(h, wi_ref[...])
    for p in range(IDX_PAIRS):
        qi_ref[0, p] = (idx[:, p * LANES:(p + 1) * LANES] * (HEAD ** -0.5)).astype(qi_ref.dtype)
    ki = idx[:, IDX_PAIRS * LANES:(IDX_PAIRS + 1) * LANES]
    ms = _seg64_sum(ki * ki) * (1.0 / HEAD)
    ki_ref[0] = (ki * lax.rsqrt(ms + EPS) * gki_ref[...]).astype(ki_ref.dtype)
    wi_out_ref[0] = idx[:, (IDX_PAIRS + 1) * LANES:] * (2 * IDX_PAIRS) ** -0.5


def dsa_proj(h, bsz, seq, w_in, q_gain, k_gain, kidx_gain, tm=256):
    t, d = h.shape
    n_idx = 2 * IDX_PAIRS
    wq = w_in[:, :d].astype(BF16)
    wk = w_in[:, d:2 * d].astype(BF16)
    wvt = w_in[:, 2 * d:3 * d].T.astype(BF16)
    o = 3 * d
    w_qi = w_in[:, o:o + n_idx * HEAD]
    w_ki = w_in[:, o + n_idx * HEAD:o + n_idx * HEAD + HEAD]
    w_wi = w_in[:, o + n_idx * HEAD + HEAD:]
    w_wi = jnp.pad(w_wi, ((0, 0), (0, LANES - n_idx)))
    wi = jnp.concatenate([w_qi, w_ki, w_ki, w_wi], axis=1).astype(BF16)
    gq = (jnp.tile(q_gain, d // HEAD) * (HEAD ** -0.5)).reshape(1, d)
    gk = jnp.tile(k_gain, d // HEAD).reshape(1, d)
    gki = jnp.tile(kidx_gain, 2).reshape(1, LANES)
    nb = seq // tm
    fixed = lambda b, i: (0, 0)
    return pl.pallas_call(
        _dsa_proj_kernel,
        out_shape=(jax.ShapeDtypeStruct((bsz, PAIRS, seq, LANES), BF16),
                   jax.ShapeDtypeStruct((bsz, PAIRS, seq, LANES), BF16),
                   jax.ShapeDtypeStruct((bsz, PAIRS, LANES, seq), BF16),
                   jax.ShapeDtypeStruct((bsz, IDX_PAIRS, seq, LANES), BF16),
                   jax.ShapeDtypeStruct((bsz, seq, LANES), BF16),
                   jax.ShapeDtypeStruct((bsz, seq, LANES), F32)),
        grid=(bsz, nb),
        in_specs=[pl.BlockSpec((tm, d), lambda b, i: (b * nb + i, 0)),
                  pl.BlockSpec((d, d), fixed), pl.BlockSpec((d, d), fixed), pl.BlockSpec((d, d), fixed),
                  pl.BlockSpec((d, wi.shape[1]), fixed),
                  pl.BlockSpec((1, d), fixed), pl.BlockSpec((1, d), fixed), pl.BlockSpec((1, LANES), fixed)],
        out_specs=(pl.BlockSpec((1, PAIRS, tm, LANES), lambda b, i: (b, 0, i, 0)),
                   pl.BlockSpec((1, PAIRS, tm, LANES), lambda b, i: (b, 0, i, 0)),
                   pl.BlockSpec((1, PAIRS, LANES, tm), lambda b, i: (b, 0, 0, i)),
                   pl.BlockSpec((1, IDX_PAIRS, tm, LANES), lambda b, i: (b, 0, i, 0)),
                   pl.BlockSpec((1, tm, LANES), lambda b, i: (b, i, 0)),
                   pl.BlockSpec((1, tm, LANES), lambda b, i: (b, i, 0))),
        compiler_params=_cp("parallel", "parallel"),
    )(h, wq, wk, wvt, wi, gq, gk, gki)


def _pair_split(x):
    lane = lax.broadcasted_iota(I32, x.shape, 1)
    zero = jnp.zeros_like(x)
    return jnp.concatenate([jnp.where(lane < HEAD, x, zero), jnp.where(lane >= HEAD, x, zero)], axis=0)


def _dsa_select_kernel(qi_ref, ki_ref, wi_ref, mask_ref, keys_ref, *, seq, topk):
    j = pl.program_id(1)
    lane = lax.broadcasted_iota(I32, (1, TQ), 1)
    limq = j * TQ + (lane // CHUNK + 1) * CHUNK
    nkt = (j * TQ + TQ + TK - 1) // TK
    wit = wi_ref[0].T
    w_pairs = [_pair_split(qi_ref[0, p]) for p in range(IDX_PAIRS)]
    kiota = lax.broadcasted_iota(I32, (TK, TQ), 0)

    def score_body(kt, carry):
        off = pl.multiple_of(kt * TK, TK)
        kit = ki_ref[0, pl.ds(off, TK), :]
        s = jnp.zeros((TK, TQ), F32)
        for p in range(IDX_PAIRS):
            lg = _nt(kit, w_pairs[p])
            s = s + jnp.maximum(lg[:, :TQ], 0.0) * wit[2 * p:2 * p + 1, :]
            s = s + jnp.maximum(lg[:, TQ:], 0.0) * wit[2 * p + 1:2 * p + 2, :]
        s = jnp.where(off + kiota < limq, s, NEG_INF)
        bits = pltpu.bitcast(s, I32)
        keys_ref[pl.ds(off, TK), :] = jnp.where(bits < 0, bits ^ 0x7FFFFFFF, bits)
        return carry

    lax.fori_loop(0, nkt, score_body, 0)

    def count(pred):
        def body(i, acc):
            off = pl.multiple_of(i * TK, TK)
            hit = jnp.where(pred(keys_ref[pl.ds(off, TK), :], off), 1, 0)
            return acc + hit.reshape(TK // 8, 8, TQ).sum(axis=0)
        acc = lax.fori_loop(0, nkt, body, jnp.zeros((8, TQ), I32))
        return acc.sum(axis=0, keepdims=True)

    def search(_):
        def bit_body(t, tu):
            cand_u = tu | jnp.left_shift(jnp.int32(1), 31 - t)
            cand_s = cand_u ^ INT_MIN
            c = count(lambda kb, off: kb >= cand_s)
            return jnp.where(c >= topk, cand_u, tu)
        tu = lax.fori_loop(0, 32, bit_body, jnp.zeros((1, TQ), I32))
        return tu ^ INT_MIN

    ts = lax.cond(j * TQ + TQ > topk, search, lambda _: jnp.full((1, TQ), INT_MIN, I32), 0)

    n_ge = count(lambda kb, off: kb >= ts)

    @pl.when(jnp.max(n_ge) > topk)
    def _break_ties():
        r = topk - count(lambda kb, off: kb > ts)

        def bit_body(t, p):
            cand = p | jnp.left_shift(jnp.int32(1), (seq.bit_length() - 1) - t)
            c = count(lambda kb, off: (kb == ts) & (off + kiota < cand))
            return jnp.where(c < r, cand, p)
        p_last = lax.fori_loop(0, seq.bit_length(), bit_body, jnp.zeros((1, TQ), I32))

        def demote(kt, carry):
            off = pl.multiple_of(kt * TK, TK)
            kb = keys_ref[pl.ds(off, TK), :]
            keys_ref[pl.ds(off, TK), :] = jnp.where((kb == ts) & (off + kiota > p_last), ts - 1, kb)
            return carry
        lax.fori_loop(0, nkt, demote, 0)

    def out_body(kt, carry):
        off = pl.multiple_of(kt * TK, TK)

        @pl.when(kt < nkt)
        def _():
            kb = keys_ref[pl.ds(off, TK), :]
            sel = (kb >= ts) & (off + kiota < limq)
            mask_ref[0, 0, pl.ds(off, TK), :] = jnp.where(sel, 1, 0).astype(mask_ref.dtype)

        @pl.when(kt >= nkt)
        def _():
            mask_ref[0, 0, pl.ds(off, TK), :] = jnp.zeros((TK, TQ), mask_ref.dtype)
        return carry

    lax.fori_loop(0, seq // TK, out_body, 0)


def dsa_select(qi, ki, wi, topk):
    bsz, _, seq, _ = qi.shape
    nq = seq // TQ
    return pl.pallas_call(
        functools.partial(_dsa_select_kernel, seq=seq, topk=topk),
        out_shape=jax.ShapeDtypeStruct((bsz, nq, seq, TQ), jnp.int8),
        grid=(bsz, nq),
        in_specs=[pl.BlockSpec((1, IDX_PAIRS, TQ, LANES), lambda b, j: (b, 0, j, 0)),
                  pl.BlockSpec((1, seq, LANES), lambda b, j: (b, 0, 0)),
                  pl.BlockSpec((1, TQ, LANES), lambda b, j: (b, j, 0))],
        out_specs=pl.BlockSpec((1, 1, seq, TQ), lambda b, j: (b, j, 0, 0)),
        scratch_shapes=[pltpu.VMEM((seq, TQ), I32)],
        compiler_params=_cp("parallel", "parallel"),
    )(qi, ki, wi)


def _dsa_attn_kernel(q_ref, k_ref, vt_ref, mask_ref, o_ref, qm_ref, m_ref, l_ref, acc_ref):
    j = pl.program_id(1)
    kt = pl.program_id(2)
    last = (j * TQ + TQ - 1) // TK

    @pl.when(kt == 0)
    def _init():
        for p in range(PAIRS):
            qm_ref[p] = _pair_split(q_ref[0, p])
        m_ref[...] = jnp.full(m_ref.shape, -jnp.inf, F32)
        l_ref[...] = jnp.zeros(l_ref.shape, F32)
        acc_ref[...] = jnp.zeros(acc_ref.shape, F32)

    @pl.when(kt <= last)
    def _compute():
        bias = (mask_ref[0, 0].astype(F32) - 1.0) * 1e30
        bias2 = jnp.concatenate([bias, bias], axis=1)

        def pair_body(p, carry):
            s = _nt(k_ref[0, p], qm_ref[p]) + bias2
            m_old = m_ref[p]
            m_new = jnp.maximum(m_old, jnp.max(s, axis=0, keepdims=True))
            alpha = jnp.exp(m_old - m_new)
            pr = jnp.exp(s - m_new)
            l_ref[p] = l_ref[p] * alpha + jnp.sum(pr, axis=0, keepdims=True)
            m_ref[p] = m_new
            acc_ref[p] = acc_ref[p] * alpha + _dot(vt_ref[0, p], pr.astype(BF16))
            return carry
        lax.fori_loop(0, PAIRS, pair_body, 0)

    @pl.when(kt == last)
    def _finish():
        for p in range(PAIRS):
            a = acc_ref[p] / l_ref[p]
            ot = jnp.concatenate([a[0:HEAD, 0:TQ], a[HEAD:2 * HEAD, TQ:2 * TQ]], axis=0)
            o_ref[0, :, p * LANES:(p + 1) * LANES] = ot.T.astype(o_ref.dtype)


def dsa_attn(q, k, vt, mask):
    bsz, _, seq, _ = q.shape
    nq, nk = seq // TQ, seq // TK
    last = lambda j: (j * TQ + TQ - 1) // TK
    return pl.pallas_call(
        _dsa_attn_kernel,
        out_shape=jax.ShapeDtypeStruct((bsz, seq, PAIRS * LANES), BF16),
        grid=(bsz, nq, nk),
        in_specs=[pl.BlockSpec((1, PAIRS, TQ, LANES), lambda b, j, t: (b, 0, j, 0)),
                  pl.BlockSpec((1, PAIRS, TK, LANES), lambda b, j, t: (b, 0, jnp.minimum(t, last(j)), 0)),
                  pl.BlockSpec((1, PAIRS, LANES, TK), lambda b, j, t: (b, 0, 0, jnp.minimum(t, last(j)))),
                  pl.BlockSpec((1, 1, TK, TQ), lambda b, j, t: (b, j, jnp.minimum(t, last(j)), 0))],
        out_specs=pl.BlockSpec((1, TQ, PAIRS * LANES), lambda b, j, t: (b, j, 0)),
        scratch_shapes=[pltpu.VMEM((PAIRS, 2 * TQ, LANES), BF16),
                        pltpu.VMEM((PAIRS, 1, 2 * TQ), F32),
                        pltpu.VMEM((PAIRS, 1, 2 * TQ), F32),
                        pltpu.VMEM((PAIRS, LANES, 2 * TQ), F32)],
        compiler_params=_cp("parallel", "parallel", "arbitrary"),
    )(q, k, vt, mask)


def dsa_mixer(h, bsz, seq, w_in, q_gain, k_gain, kidx_gain):
    assert seq % TK == 0 and TK % TQ == 0
    topk = min(TOPK_MAX, seq // 4)
    q, k, vt, qi, ki, wi = dsa_proj(h, bsz, seq, w_in, q_gain, k_gain, kidx_gain)
    mask = dsa_select(qi, ki, wi, topk)
    return dsa_attn(q, k, vt, mask).reshape(bsz * seq, PAIRS * LANES)


RW_PREV = 8
RW_C = 64
RW_CH = 4


def _rwkv_proj_kernel(x_ref, xp_ref, gn_ref, mu_ref, wr_ref, wk_ref, wv_ref, w1_ref, w2_ref, a1_ref, a2_ref,
                      g1_ref, g2_ref, w0_ref, a0_ref, kk_ref, ka_ref,
                      r_out, lw_out, k_out, v_out, kk_out, b_out, g_out, *, tiles_per_seq):
    i = pl.program_id(0)
    gn = gn_ref[...]
    h = _rms(x_ref[...], gn)
    hp = _rms(xp_ref[...], gn)
    hp_last = jnp.where((i % tiles_per_seq) == 0, 0.0, hp[RW_PREV - 1:RW_PREV, :])
    rows = lax.broadcasted_iota(I32, h.shape, 0)
    dh = jnp.where(rows == 0, hp_last, pltpu.roll(h, 1, 0)) - h
    mu = mu_ref[...]
    xs = lambda n: (h + dh * mu[n:n + 1, :]).astype(BF16)
    r = _dot(xs(0), wr_ref[...])
    k = _dot(xs(1), wk_ref[...])
    v_out[...] = _dot(xs(2), wv_ref[...])
    wl = w0_ref[...] + _dot(jnp.tanh(_dot(xs(3), w1_ref[...])).astype(BF16), w2_ref[...])
    z = -wl
    w_log = -(jnp.maximum(z, 0.0) + jnp.log(1.0 + jnp.exp(-jnp.abs(z)))) - 0.5
    lw_out[...] = -jnp.exp(w_log)
    a = jax.nn.sigmoid(a0_ref[...] + _dot(_dot(xs(4), a1_ref[...]).astype(BF16), a2_ref[...]))
    g_out[...] = _dot(jax.nn.sigmoid(_dot(xs(5), g1_ref[...])).astype(BF16), g2_ref[...])
    r_out[...] = r
    k_out[...] = k * (1.0 + (a - 1.0) * ka_ref[...])
    kk = k * kk_ref[...]
    for p in range(kk.shape[1] // LANES):
        sl = slice(p * LANES, (p + 1) * LANES)
        kp = kk[:, sl]
        kn = kp / jnp.maximum(jnp.sqrt(_seg64_sum(kp * kp)), 1e-12)
        kk_out[:, sl] = kn
        b_out[:, sl] = kn * a[:, sl]


def rwkv_proj(x2, seq, gn, mu, w_rkv, w0, w1, w2, a0, a1, a2, g1, g2, k_k, k_a, tm=256):
    t, d = x2.shape
    pad_c = lambda w: jnp.pad(w, ((0, 0), (0, LANES - w.shape[1]))).astype(BF16)
    pad_r = lambda w: jnp.pad(w, ((0, LANES - w.shape[0]), (0, 0))).astype(BF16)
    row = lambda i: (i, 0)
    fixed = lambda i: (0, 0)
    vec = lambda a: a.reshape(1, d)
    full = lambda a: pl.BlockSpec(a.shape, fixed)
    args = [x2, x2, vec(gn), jnp.pad(mu, ((0, 2), (0, 0))),
            w_rkv[0].astype(BF16), w_rkv[1].astype(BF16), w_rkv[2].astype(BF16),
            pad_c(w1), pad_r(w2), pad_c(a1), pad_r(a2), g1.astype(BF16), g2.astype(BF16),
            vec(w0), vec(a0), vec(k_k), vec(k_a)]
    in_specs = [pl.BlockSpec((tm, d), row),
                pl.BlockSpec((RW_PREV, d), lambda i: (jnp.maximum(i * (tm // RW_PREV) - 1, 0), 0))]
    in_specs += [full(a) for a in args[2:]]
    return pl.pallas_call(
        functools.partial(_rwkv_proj_kernel, tiles_per_seq=seq // tm),
        out_shape=tuple(jax.ShapeDtypeStruct((t, d), F32) for _ in range(7)),
        grid=(t // tm,),
        in_specs=in_specs,
        out_specs=tuple(pl.BlockSpec((tm, d), row) for _ in range(7)),
        compiler_params=_cp("parallel"),
    )(*args)


def _bdot(a, b):
    return _dot(a.astype(BF16), b.astype(BF16))


def _rwkv_chunk_terms(r, lw, k2, v, kk, b):
    c = RW_C
    cat = jnp.concatenate
    row = lax.broadcasted_iota(I32, (2 * c, 2 * c), 0)
    col = lax.broadcasted_iota(I32, (2 * c, 2 * c), 1)
    r_c = lax.broadcasted_iota(I32, (c, c), 0)
    c_c = lax.broadcasted_iota(I32, (c, c), 1)
    tri = jnp.where(c_c <= r_c, 1.0, 0.0).astype(BF16)
    hi = lw.astype(BF16)
    rem = lw - hi.astype(F32)
    mid = rem.astype(BF16)
    lo = (rem - mid.astype(F32)).astype(BF16)
    cs = _dot(tri, hi) + _dot(tri, mid) + _dot(tri, lo)
    e_pos = jnp.exp(cs)
    e_neg = jnp.exp(-cs)
    rt = r * e_pos
    kt = k2 * e_neg
    bt = b * e_neg
    kkt = kk * jnp.exp(cs - lw)
    g_end = e_pos[c - 1:c, :]
    khat = kt * g_end
    bhat = bt * g_end
    lane = lax.broadcasted_iota(I32, (c, LANES), 1)
    t_row = lax.broadcasted_iota(I32, (c, LANES), 0)
    m0 = lane < HEAD
    h0 = lambda x: jnp.where(m0, x, 0.0)
    h1 = lambda x: jnp.where(m0, 0.0, x)
    g0 = _nt(cat([h0(kkt), h0(rt)]).astype(BF16), cat([bt, kt]).astype(BF16))
    g1 = _nt(cat([h1(rt), h1(kkt)]).astype(BF16), cat([kt, bt]).astype(BF16))
    top, left = row < c, col < c
    a_b = jnp.where(top & left & (col < row), g0, 0.0) + jnp.where(~top & ~left & (col < row), g1, 0.0)
    a_k = jnp.where(top & ~left & (col - c < row), g0, 0.0) + jnp.where(~top & left & (col < row - c), g1, 0.0)
    x = jnp.where(row == col, 1.0, 0.0) - a_b
    pw = _bdot(a_b, a_b)
    for it in range(5):
        x = x + _bdot(x, pw)
        if it < 4:
            pw = _bdot(pw, pw)
    v0, v1 = h0(v), h1(v)
    akv = _bdot(a_k, cat([v1, v0]))
    wu = _bdot(x, cat([cat([h0(kkt), h1(kkt)]), akv], axis=1))
    incl = jnp.where(m0, lane, lane - HEAD) <= t_row
    zeros = jnp.zeros((c, LANES), F32)
    rhs0 = cat([-wu[0:c], cat([zeros, v0], axis=1)])
    rhs1 = cat([cat([zeros, v1], axis=1), -wu[c:2 * c]])
    o0 = _bdot(jnp.where(incl, g0[c:2 * c], 0.0), rhs0)
    o1 = _bdot(jnp.where(incl, g1[0:c], 0.0), rhs1)
    rq = rt + o0[:, :LANES] + o1[:, :LANES]
    yin = o0[:, LANES:] + o1[:, LANES:]
    lhs = cat([h0(bhat), h0(khat), h1(khat), h1(bhat)])
    pd = _tn(lhs.astype(BF16), cat([rhs0, rhs1]).astype(BF16))
    phi = pd[:, :LANES] + jnp.where(row == col, jnp.broadcast_to(g_end, (2 * c, LANES)), 0.0)
    return rq, yin, phi, pd[:, LANES:]


def _rwkv_scan_kernel(r_ref, lw_ref, k_ref, v_ref, kk_ref, b_ref, y_ref, h_ref):
    @pl.when(pl.program_id(2) == 0)
    def _():
        h_ref[...] = jnp.zeros(h_ref.shape, F32)

    def body(ci, carry):
        sl = pl.ds(pl.multiple_of(ci * RW_C, RW_C), RW_C)
        rq, yin, phi, dh = _rwkv_chunk_terms(r_ref[0, sl, :], lw_ref[0, sl, :], k_ref[0, sl, :],
                                             v_ref[0, sl, :], kk_ref[0, sl, :], b_ref[0, sl, :])
        hb = h_ref[...].astype(BF16)
        both = _dot(jnp.concatenate([rq, phi]).astype(BF16), hb)
        y_ref[0, sl, :] = both[0:RW_C] + yin
        h_ref[...] = both[RW_C:] + dh
        return carry
    lax.fori_loop(0, RW_CH, body, 0)


def rwkv_scan(r, lw, k2, v, kk, b, bsz, seq):
    d = r.shape[-1]
    rows = RW_C * RW_CH
    spec = pl.BlockSpec((1, rows, LANES), lambda bb, p, c: (bb, c, p))
    shp = lambda a: a.reshape(bsz, seq, d)
    return pl.pallas_call(
        _rwkv_scan_kernel,
        out_shape=jax.ShapeDtypeStruct((bsz, seq, d), F32),
        grid=(bsz, d // LANES, seq // rows),
        in_specs=[spec] * 6,
        out_specs=spec,
        scratch_shapes=[pltpu.VMEM((LANES, LANES), F32)],
        compiler_params=_cp("parallel", "parallel", "arbitrary"),
    )(shp(r), shp(lw), shp(k2), shp(v), shp(kk), shp(b)).reshape(bsz * seq, d)


def _rwkv_post_kernel(y_ref, r_ref, k_ref, v_ref, g_ref, lng_ref, lnb_ref, rk_ref, w_ref, x_ref, gn_ref,
                      xo_ref, ho_ref, o_scr):
    for p in range(y_ref.shape[1] // LANES):
        sl = slice(p * LANES, (p + 1) * LANES)
        y = y_ref[:, sl]
        dv = y - _seg64_sum(y) * (1.0 / HEAD)
        var = _seg64_sum(dv * dv) * (1.0 / HEAD)
        yn = dv * lax.rsqrt(var + GN_EPS) * lng_ref[:, sl] + lnb_ref[:, sl]
        bonus = _seg64_sum(r_ref[:, sl] * k_ref[:, sl] * rk_ref[:, sl]) * v_ref[:, sl]
        o_scr[:, sl] = ((yn + bonus) * g_ref[:, sl]).astype(o_scr.dtype)
    xn = x_ref[...] + _dot(o_scr[...], w_ref[...])
    xo_ref[...] = xn
    ho_ref[...] = _rms(xn, gn_ref[...]).astype(ho_ref.dtype)


def rwkv_post(y, r, k2, v, g, ln_g, ln_b, r_k, w_out, x2, gn, tm=256):
    t, d = x2.shape
    row = lambda i: (i, 0)
    fixed = lambda i: (0, 0)
    tile = pl.BlockSpec((tm, d), row)
    vecs = pl.BlockSpec((1, d), fixed)
    return pl.pallas_call(
        _rwkv_post_kernel,
        out_shape=(jax.ShapeDtypeStruct((t, d), F32), jax.ShapeDtypeStruct((t, d), BF16)),
        grid=(t // tm,),
        in_specs=[tile] * 5 + [vecs] * 3 + [pl.BlockSpec((d, d), fixed), tile, vecs],
        out_specs=(tile, tile),
        scratch_shapes=[pltpu.VMEM((tm, d), BF16)],
        compiler_params=_cp("parallel"),
    )(y, r, k2, v, g, ln_g.reshape(1, d), ln_b.reshape(1, d), r_k.reshape(1, d), w_out.astype(BF16), x2,
      gn.reshape(1, d))


S5_L = 16


def _split(a):
    hi = a.astype(BF16)
    return hi, (a - hi.astype(F32)).astype(BF16)


def _dot3(a, b_hi, b_lo):
    a_hi, a_lo = _split(a)
    return _dot(a_hi, b_hi) + _dot(a_lo, b_hi) + _dot(a_hi, b_lo)


def _s5_local_kernel(u_ref, grh_ref, grl_ref, gih_ref, gil_ref, xr_ref, xi_ref):
    u0, u1 = u_ref[0], u_ref[1]
    xr_ref[...] = _dot3(u0, grh_ref[0, 0], grl_ref[0, 0]) + _dot3(u1, grh_ref[0, 1], grl_ref[0, 1])
    xi_ref[...] = _dot3(u0, gih_ref[0, 0], gil_ref[0, 0]) + _dot3(u1, gih_ref[0, 1], gil_ref[0, 1])


def _s5_carry_kernel(er_ref, ei_ref, lr_ref, li_ref, pr_ref, pi_ref):
    lr = lr_ref[...]
    li = li_ref[...]

    def body(n, st):
        sr, si = st
        pr_ref[n] = sr
        pi_ref[n] = si
        return (lr * sr - li * si + er_ref[n], lr * si + li * sr + ei_ref[n])
    zero = jnp.zeros(lr.shape, F32)
    lax.fori_loop(0, er_ref.shape[0], body, (zero, zero))


def _s5_out_kernel(u_ref, pr_ref, pi_ref, kh_ref, kl_ref, erh_ref, erl_ref, eih_ref, eil_ref, y_ref):
    pr = pr_ref[...]
    pi = pi_ref[...]
    for i in range(2):
        y_ref[i] = (_dot3(u_ref[i], kh_ref[i], kl_ref[i]) + _dot3(pr, erh_ref[0, i], erl_ref[0, i])
                    + _dot3(pi, eih_ref[0, i], eil_ref[0, i]))


def s5_ssm(h3, a_re, a_im, log_step, b_re, b_im, c_re, c_im):
    bsz, seq, d = h3.shape
    ng, ns = a_re.shape
    gc = d // ng
    L = S5_L
    nc = seq // L
    n = bsz * nc
    step = jnp.exp(log_step.astype(F32))[:, None]
    lam = lax.complex(a_re.astype(F32), a_im.astype(F32))
    lam_bar = jnp.exp(lam * step)
    b_bar = ((lam_bar - 1.0) / lam)[..., None] * lax.complex(b_re.astype(F32), b_im.astype(F32))
    cc = lax.complex(c_re.astype(F32), c_im.astype(F32))
    pw = jnp.exp((lam * step)[:, None, :] * jnp.arange(L + 1, dtype=F32)[None, :, None])
    lag = jnp.arange(L)[None, :] - jnp.arange(L)[:, None]
    kfull = jnp.einsum('gcp,gstp,gpe->gsetc', cc, pw[:, jnp.clip(lag, 0, L)], b_bar)
    kmat = jnp.where((lag >= 0)[None, :, None, :, None], jnp.real(kfull), 0.0).reshape(ng, L * gc, L * gc)
    gfull = jnp.einsum('gsp,gpe->gsep', pw[:, L - 1 - jnp.arange(L)], b_bar).reshape(ng, L * gc, ns)
    efull = jnp.einsum('gcp,gtp->gptc', cc, pw[:, 1:]).reshape(ng, ns, L * gc)
    lam_l = pw[:, L]
    assert 2 * ns == LANES and ng % 16 == 0
    nq = ng // 2

    def cols(m):
        m4 = m.reshape(nq, 2, m.shape[1], ns)
        z = jnp.zeros_like(m4[:, 0])
        return jnp.stack([jnp.concatenate([m4[:, 0], z], -1), jnp.concatenate([z, m4[:, 1]], -1)], 1)

    def rows(m):
        m4 = m.reshape(nq, 2, ns, m.shape[2])
        z = jnp.zeros_like(m4[:, 0])
        return jnp.stack([jnp.concatenate([m4[:, 0], z], -2), jnp.concatenate([z, m4[:, 1]], -2)], 1)

    kh, kl = _split(kmat)
    grh, grl = _split(cols(jnp.real(gfull)))
    gih, gil = _split(cols(jnp.imag(gfull)))
    erh, erl = _split(rows(jnp.real(efull)))
    eih, eil = _split(rows(-jnp.imag(efull)))
    lr = jnp.real(lam_l).reshape(nq, LANES)
    li = jnp.imag(lam_l).reshape(nq, LANES)

    lw = L * gc
    u = h3.reshape(bsz, nc, L, ng, gc).transpose(3, 0, 1, 2, 4).reshape(ng, n, lw)
    pair3 = lambda q: (q, 0, 0)
    pair4 = lambda q: (q, 0, 0, 0)
    col = lambda q: (0, q)
    gspec = pl.BlockSpec((1, 2, lw, LANES), pair4)
    xr, xi = pl.pallas_call(
        _s5_local_kernel,
        out_shape=(jax.ShapeDtypeStruct((n, nq * LANES), F32),) * 2,
        grid=(nq,),
        in_specs=[pl.BlockSpec((2, n, lw), pair3), gspec, gspec, gspec, gspec],
        out_specs=(pl.BlockSpec((n, LANES), col),) * 2,
        compiler_params=_cp("parallel"),
    )(u, grh, grl, gih, gil)
    st_spec = pl.BlockSpec((nc, 8, LANES), lambda b, j: (b, j, 0))
    lam_spec = pl.BlockSpec((8, LANES), lambda b, j: (j, 0))
    pr, pi = pl.pallas_call(
        _s5_carry_kernel,
        out_shape=(jax.ShapeDtypeStruct((n, nq, LANES), F32),) * 2,
        grid=(bsz, nq // 8),
        in_specs=[st_spec, st_spec, lam_spec, lam_spec],
        out_specs=(st_spec, st_spec),
        compiler_params=_cp("parallel", "parallel"),
    )(xr.reshape(n, nq, LANES), xi.reshape(n, nq, LANES), lr, li)
    espec = pl.BlockSpec((1, 2, LANES, lw), pair4)
    kspec = pl.BlockSpec((2, lw, lw), pair3)
    y = pl.pallas_call(
        _s5_out_kernel,
        out_shape=jax.ShapeDtypeStruct((ng, n, lw), F32),
        grid=(nq,),
        in_specs=[pl.BlockSpec((2, n, lw), pair3), pl.BlockSpec((n, LANES), col), pl.BlockSpec((n, LANES), col),
                  kspec, kspec, espec, espec, espec, espec],
        out_specs=pl.BlockSpec((2, n, lw), pair3),
        compiler_params=_cp("parallel"),
    )(u, pr.reshape(n, nq * LANES), pi.reshape(n, nq * LANES), kh, kl, erh, erl, eih, eil)
    return y.reshape(ng, bsz, nc, L, gc).transpose(1, 2, 3, 0, 4).reshape(bsz * seq, d)


def _s5_glu_kernel(ys_ref, h_ref, d_ref, w_ref, b_ref, x_ref, gn_ref, xo_ref, ho_ref):
    y = ys_ref[...] + d_ref[...] * h_ref[...]
    gelu = 0.5 * y * (1.0 + jnp.tanh(math.sqrt(2.0 / math.pi) * (y + 0.044715 * (y * y * y))))
    z = _dot(gelu.astype(BF16), w_ref[...]) + b_ref[...]
    dm = x_ref.shape[1]
    xn = x_ref[...] + z[:, :dm] * jax.nn.sigmoid(z[:, dm:])
    xo_ref[...] = xn
    ho_ref[...] = _rms(xn, gn_ref[...]).astype(ho_ref.dtype)


def s5_glu(ys, h, d_skip, w_glu, b_glu, x2, gn, tm=256):
    t, d = x2.shape
    row = lambda i: (i, 0)
    fixed = lambda i: (0, 0)
    tile = pl.BlockSpec((tm, d), row)
    return pl.pallas_call(
        _s5_glu_kernel,
        out_shape=(jax.ShapeDtypeStruct((t, d), F32), jax.ShapeDtypeStruct((t, d), BF16)),
        grid=(t // tm,),
        in_specs=[tile, tile, pl.BlockSpec((1, d), fixed), pl.BlockSpec((d, 2 * d), fixed),
                  pl.BlockSpec((1, 2 * d), fixed), tile, pl.BlockSpec((1, d), fixed)],
        out_specs=(tile, tile),
        compiler_params=_cp("parallel"),
    )(ys, h, d_skip.reshape(1, d), w_glu.astype(BF16), b_glu.reshape(1, 2 * d), x2, gn.reshape(1, d))


def kernel(x, norm_mix, norm_ffn, dsa_w_in, dsa_q_norm, dsa_k_norm, dsa_kidx_norm, dsa_w_out, rwkv_mu, rwkv_w_rkv, rwkv_w0, rwkv_w1, rwkv_w2, rwkv_a0, rwkv_a1, rwkv_a2, rwkv_g1, rwkv_g2, rwkv_k_k, rwkv_k_a, rwkv_r_k, rwkv_ln_g, rwkv_ln_b, rwkv_w_out, s5_a_re, s5_a_im, s5_log_step, s5_b_re, s5_b_im, s5_c_re, s5_c_im, s5_d, s5_w_glu, s5_b_glu, ffn_w_up, ffn_conv_w, ffn_conv_b, ffn_w_down):
    bsz, seq, d = x.shape
    depth = norm_mix.shape[0]
    x2 = x.reshape(bsz * seq, d)
    for i in range(depth):
        kind, j = i % 3, i // 3
        if kind == 0:
            h = rmsnorm(x2, norm_mix[i], BF16)
            o = dsa_mixer(h, bsz, seq, dsa_w_in[j], dsa_q_norm[j], dsa_k_norm[j], dsa_kidx_norm[j])
            x2, h = mm_res_norm(o, dsa_w_out[j].astype(BF16), x2, norm_ffn[i], BF16)
        elif kind == 1:
            r, lw, k2, v, kk, b, g = rwkv_proj(x2, seq, norm_mix[i], rwkv_mu[j], rwkv_w_rkv[j], rwkv_w0[j],
                                               rwkv_w1[j], rwkv_w2[j], rwkv_a0[j], rwkv_a1[j], rwkv_a2[j],
                                               rwkv_g1[j], rwkv_g2[j], rwkv_k_k[j], rwkv_k_a[j])
            y = rwkv_scan(r, lw, k2, v, kk, b, bsz, seq)
            x2, h = rwkv_post(y, r, k2, v, g, rwkv_ln_g[j], rwkv_ln_b[j], rwkv_r_k[j].reshape(d), rwkv_w_out[j],
                              x2, norm_ffn[i])
        else:
            hf = rmsnorm(x2, norm_mix[i], F32)
            ys = s5_ssm(hf.reshape(bsz, seq, d), s5_a_re[j], s5_a_im[j], s5_log_step[j], s5_b_re[j], s5_b_im[j],
                        s5_c_re[j], s5_c_im[j])
            x2, h = s5_glu(ys, hf, s5_d[j], s5_w_glu[j], s5_b_glu[j], x2, norm_ffn[i])
        act = ffn_up(h, ffn_w_up[i].astype(BF16), ffn_conv_w[i], ffn_conv_b[i], seq)
        x2, _ = mm_res_norm(act, ffn_w_down[i].astype(BF16), x2, None, None)
    return x2.reshape(bsz, seq, d)
```

```python
import functools
import math

import jax
import jax.numpy as jnp
from jax import lax
from jax.experimental import pallas as pl
from jax.experimental.pallas import tpu as pltpu

F32 = jnp.float32
BF16 = jnp.bfloat16
I32 = jnp.int32

EPS = 1e-6
NEG_INF = -1e30
LANES = 128
HEAD = 64
CHUNK = 64
TOPK_MAX = 256
GN_EPS = 64e-5
INT_MIN = -(2 ** 31)

VMEM_LIMIT = 56 * 1024 * 1024


def _cp(*sem):
    return pltpu.CompilerParams(dimension_semantics=sem, vmem_limit_bytes=VMEM_LIMIT)


def _nt(a, b):
    return lax.dot_general(a, b, (((1,), (1,)), ((), ())), preferred_element_type=F32)


def _tn(a, b):
    return lax.dot_general(a, b, (((0,), (0,)), ((), ())), preferred_element_type=F32)


def _dot(a, b):
    return jnp.dot(a, b, preferred_element_type=F32)


def _rms(x, g):
    ms = jnp.mean(x * x, axis=-1, keepdims=True)
    return x * lax.rsqrt(ms + EPS) * g


SEG_W = 2 * LANES


def _seg64_ones():
    r = lax.broadcasted_iota(I32, (SEG_W, SEG_W), 0) // HEAD
    c = lax.broadcasted_iota(I32, (SEG_W, SEG_W), 1) // HEAD
    return jnp.where(r == c, 1.0, 0.0).astype(BF16)


def _seg64_sum(x, ones):
    hi = x.astype(BF16)
    lo = (x - hi.astype(F32)).astype(BF16)
    return _dot(hi, ones) + _dot(lo, ones)


def _norm_kernel(x_ref, g_ref, o_ref):
    o_ref[...] = _rms(x_ref[...], g_ref[...]).astype(o_ref.dtype)


def rmsnorm(x2, g, out_dtype, tm=1024):
    t, d = x2.shape
    return pl.pallas_call(
        _norm_kernel,
        out_shape=jax.ShapeDtypeStruct((t, d), out_dtype),
        grid=(t // tm,),
        in_specs=[pl.BlockSpec((tm, d), lambda i: (i, 0)),
                  pl.BlockSpec((1, d), lambda i: (0, 0))],
        out_specs=pl.BlockSpec((tm, d), lambda i: (i, 0)),
        compiler_params=_cp("parallel"),
    )(x2, g.reshape(1, d))


def _mm_res_norm_kernel(a_ref, w_ref, x_ref, g_ref, xo_ref, ho_ref):
    xn = x_ref[...] + _dot(a_ref[...], w_ref[...])
    xo_ref[...] = xn
    ho_ref[...] = _rms(xn, g_ref[...]).astype(ho_ref.dtype)


def _mm_res_kernel(a_ref, w_ref, x_ref, xo_ref):
    xo_ref[...] = x_ref[...] + _dot(a_ref[...], w_ref[...])


def mm_res_norm(a, w, x2, g, h_dtype, tm=512):
    t, k = a.shape
    d = w.shape[1]
    row = lambda i: (i, 0)
    fixed = lambda i: (0, 0)
    in_specs = [pl.BlockSpec((tm, k), row), pl.BlockSpec((k, d), fixed), pl.BlockSpec((tm, d), row)]
    if g is None:
        return pl.pallas_call(
            _mm_res_kernel,
            out_shape=jax.ShapeDtypeStruct((t, d), F32),
            grid=(t // tm,), in_specs=in_specs, out_specs=pl.BlockSpec((tm, d), row),
            compiler_params=_cp("parallel"),
        )(a, w, x2), None
    return pl.pallas_call(
        _mm_res_norm_kernel,
        out_shape=(jax.ShapeDtypeStruct((t, d), F32), jax.ShapeDtypeStruct((t, d), h_dtype)),
        grid=(t // tm,),
        in_specs=in_specs + [pl.BlockSpec((1, d), fixed)],
        out_specs=(pl.BlockSpec((tm, d), row), pl.BlockSpec((tm, d), row)),
        compiler_params=_cp("parallel"),
    )(a, w, x2, g.reshape(1, d))


PREV_ROWS = 16


def _ffn_up_kernel(h_ref, hp_ref, wg_ref, wv_ref, cwg_ref, cwv_ref, cbg_ref, cbv_ref, o_ref, *, tiles_per_seq):
    i = pl.program_id(1)
    h = h_ref[...]
    hp = hp_ref[...]
    seq_start = (i % tiles_per_seq) == 0

    def branch(w_ref, cw_ref, cb_ref):
        w = w_ref[...]
        u = _dot(h, w)
        up = jnp.where(seq_start, 0.0, _dot(hp, w))
        rows = lax.broadcasted_iota(I32, u.shape, 0)
        last = up[PREV_ROWS - 1:PREV_ROWS, :]
        u1 = jnp.where(rows == 0, last, pltpu.roll(u, 1, 0))
        u2 = jnp.where(rows == 0, up[PREV_ROWS - 2:PREV_ROWS - 1, :],
                       jnp.where(rows == 1, last, pltpu.roll(u, 2, 0)))
        cw = cw_ref[...]
        return cw[0:1, :] * u2 + cw[1:2, :] * u1 + cw[2:3, :] * u + cb_ref[...]

    gate = branch(wg_ref, cwg_ref, cbg_ref)
    val = branch(wv_ref, cwv_ref, cbv_ref)
    o_ref[...] = (gate * jax.nn.sigmoid(gate) * val).astype(o_ref.dtype)


def ffn_up(h, w_up, conv_w, conv_b, seq, tm=512):
    t, d = h.shape
    f = w_up.shape[1] // 2
    tn = f // 2
    assert tn % LANES == 0 and seq % tm == 0
    nj = f // tn
    cw = jnp.zeros((8, 2 * f), F32).at[:conv_w.shape[0]].set(conv_w)
    cb = conv_b.reshape(1, 2 * f)
    prev = lambda j, i: (jnp.maximum(i * (tm // PREV_ROWS) - 1, 0), 0)
    return pl.pallas_call(
        functools.partial(_ffn_up_kernel, tiles_per_seq=seq // tm),
        out_shape=jax.ShapeDtypeStruct((t, f), BF16),
        grid=(nj, t // tm),
        in_specs=[pl.BlockSpec((tm, d), lambda j, i: (i, 0)),
                  pl.BlockSpec((PREV_ROWS, d), prev),
                  pl.BlockSpec((d, tn), lambda j, i: (0, j)),
                  pl.BlockSpec((d, tn), lambda j, i: (0, j + nj)),
                  pl.BlockSpec((8, tn), lambda j, i: (0, j)),
                  pl.BlockSpec((8, tn), lambda j, i: (0, j + nj)),
                  pl.BlockSpec((1, tn), lambda j, i: (0, j)),
                  pl.BlockSpec((1, tn), lambda j, i: (0, j + nj))],
        out_specs=pl.BlockSpec((tm, tn), lambda j, i: (i, j)),
        compiler_params=_cp("parallel", "arbitrary"),
    )(h, h, w_up, w_up, cw, cw, cb, cb)


PAIRS = 8
IDX_PAIRS = 4
TQ = 128
TK = 512


def _dsa_proj_kernel(h_ref, wq_ref, wk_ref, wvt_ref, wi_ref, gq_ref, gk_ref, gki_ref,
                     q_ref, k_ref, vt_ref, qi_ref, ki_ref, wi_out_ref):
    h = h_ref[...]
    tm = h.shape[0]

    ones = _seg64_ones()

    def head_norm(y, g_ref, o_ref):
        for p2 in range(PAIRS // 2):
            sl = slice(p2 * SEG_W, (p2 + 1) * SEG_W)
            yp = y[:, sl]
            ms = _seg64_sum(yp * yp, ones) * (1.0 / HEAD)
            yn = (yp * lax.rsqrt(ms + EPS) * g_ref[:, sl]).astype(o_ref.dtype)
            o_ref[0, 2 * p2] = yn[:, :LANES]
            o_ref[0, 2 * p2 + 1] = yn[:, LANES:]

    head_norm(_dot(h, wq_ref[...]), gq_ref, q_ref)
    head_norm(_dot(h, wk_ref[...]), gk_ref, k_ref)
    vt = _nt(wvt_ref[...], h)
    vt_ref[0, :, :LANES, :] = vt.reshape(PAIRS, LANES, tm).astype(vt_ref.dtype)
    vt_ref[0, :, LANES:, :] = jnp.ones((PAIRS, vt_ref.shape[2] - LANES, tm), vt_ref.dtype)
    idx = _dot(h, wi_ref[...])
    for p in range(IDX_PAIRS):
        qi_ref[0, p] = (idx[:, p * LANES:(p + 1) * LANES] * (HEAD ** -0.5)).astype(qi_ref.dtype)
    kw = idx[:, IDX_PAIRS * LANES:]
    ms = _seg64_sum(kw * kw, ones)[:, :LANES] * (1.0 / HEAD)
    ki_ref[0] = (kw[:, :LANES] * lax.rsqrt(ms + EPS) * gki_ref[...]).astype(ki_ref.dtype)
    wi_out_ref[0] = idx[:, (IDX_PAIRS + 1) * LANES:] * (2 * IDX_PAIRS) ** -0.5


def dsa_proj(h, bsz, seq, w_in, q_gain, k_gain, kidx_gain, tm=256):
    t, d = h.shape
    n_idx = 2 * IDX_PAIRS
    wq = w_in[:, :d].astype(BF16)
    wk = w_in[:, d:2 * d].astype(BF16)
    wvt = w_in[:, 2 * d:3 * d].T.astype(BF16)
    o = 3 * d
    w_qi = w_in[:, o:o + n_idx * HEAD]
    w_ki = w_in[:, o + n_idx * HEAD:o + n_idx * HEAD + HEAD]
    w_wi = w_in[:, o + n_idx * HEAD + HEAD:]
    w_wi = jnp.pad(w_wi, ((0, 0), (0, LANES - n_idx)))
    wi = jnp.concatenate([w_qi, w_ki, w_ki, w_wi], axis=1).astype(BF16)
    gq = (jnp.tile(q_gain, d // HEAD) * (HEAD ** -0.5 * math.log2(math.e))).reshape(1, d)
    gk = jnp.tile(k_gain, d // HEAD).reshape(1, d)
    gki = jnp.tile(kidx_gain, 2).reshape(1, LANES)
    nb = seq // tm
    fixed = lambda b, i: (0, 0)
    return pl.pallas_call(
        _dsa_proj_kernel,
        out_shape=(jax.ShapeDtypeStruct((bsz, PAIRS, seq, LANES), BF16),
                   jax.ShapeDtypeStruct((bsz, PAIRS, seq, LANES), BF16),
                   jax.ShapeDtypeStruct((bsz, PAIRS, VT_ROWS, seq), BF16),
                   jax.ShapeDtypeStruct((bsz, IDX_PAIRS, seq, LANES), BF16),
                   jax.ShapeDtypeStruct((bsz, seq, LANES), BF16),
                   jax.ShapeDtypeStruct((bsz, seq, LANES), F32)),
        grid=(bsz, nb),
        in_specs=[pl.BlockSpec((tm, d), lambda b, i: (b * nb + i, 0)),
                  pl.BlockSpec((d, d), fixed), pl.BlockSpec((d, d), fixed), pl.BlockSpec((d, d), fixed),
                  pl.BlockSpec((d, wi.shape[1]), fixed),
                  pl.BlockSpec((1, d), fixed), pl.BlockSpec((1, d), fixed), pl.BlockSpec((1, LANES), fixed)],
        out_specs=(pl.BlockSpec((1, PAIRS, tm, LANES), lambda b, i: (b, 0, i, 0)),
                   pl.BlockSpec((1, PAIRS, tm, LANES), lambda b, i: (b, 0, i, 0)),
                   pl.BlockSpec((1, PAIRS, VT_ROWS, tm), lambda b, i: (b, 0, 0, i)),
                   pl.BlockSpec((1, IDX_PAIRS, tm, LANES), lambda b, i: (b, 0, i, 0)),
                   pl.BlockSpec((1, tm, LANES), lambda b, i: (b, i, 0)),
                   pl.BlockSpec((1, tm, LANES), lambda b, i: (b, i, 0))),
        compiler_params=_cp("parallel", "parallel"),
    )(h, wq, wk, wvt, wi, gq, gk, gki)


def _pair_split(x):
    lane = lax.broadcasted_iota(I32, x.shape, 1)
    zero = jnp.zeros_like(x)
    return jnp.concatenate([jnp.where(lane < HEAD, x, zero), jnp.where(lane >= HEAD, x, zero)], axis=0)


def _dsa_select_kernel(qi_ref, ki_ref, wi_ref, mask_ref, keys_ref, *, seq, topk):
    j = pl.program_id(1)
    lane = lax.broadcasted_iota(I32, (1, TQ), 1)
    limq = j * TQ + (lane // CHUNK + 1) * CHUNK
    nkt = (j * TQ + TQ + TK - 1) // TK
    wit = wi_ref[0].T
    w_pairs = [_pair_split(qi_ref[0, p]) for p in range(IDX_PAIRS)]
    kiota = lax.broadcasted_iota(I32, (TK, TQ), 0)

    def score_body(kt, carry):
        off = pl.multiple_of(kt * TK, TK)
        kit = ki_ref[0, pl.ds(off, TK), :]
        s = jnp.zeros((TK, TQ), F32)
        for p in range(IDX_PAIRS):
            lg = _nt(kit, w_pairs[p])
            s = s + jnp.maximum(lg[:, :TQ], 0.0) * wit[2 * p:2 * p + 1, :]
            s = s + jnp.maximum(lg[:, TQ:], 0.0) * wit[2 * p + 1:2 * p + 2, :]
        s = jnp.where(off + kiota < limq, s, NEG_INF)
        bits = pltpu.bitcast(s, I32)
        keys_ref[pl.ds(off, TK), :] = jnp.where(bits < 0, bits ^ 0x7FFFFFFF, bits)
        return carry

    lax.fori_loop(0, nkt, score_body, 0)

    def count(pred):
        def body(i, acc):
            off = pl.multiple_of(i * TK, TK)
            hit = jnp.where(pred(keys_ref[pl.ds(off, TK), :], off), 1, 0)
            return acc + hit.reshape(TK // 8, 8, TQ).sum(axis=0)
        acc = lax.fori_loop(0, nkt, body, jnp.zeros((8, TQ), I32))
        return acc.sum(axis=0, keepdims=True)

    def search(_):
        def bit_body(t, tu):
            cand_u = tu | jnp.left_shift(jnp.int32(1), 31 - t)
            cand_s = cand_u ^ INT_MIN
            c = count(lambda kb, off: kb >= cand_s)
            return jnp.where(c >= topk, cand_u, tu)
        tu = lax.fori_loop(0, 32, bit_body, jnp.zeros((1, TQ), I32))
        return tu ^ INT_MIN

    ts = lax.cond(j * TQ + TQ > topk, search, lambda _: jnp.full((1, TQ), INT_MIN, I32), 0)

    n_ge = count(lambda kb, off: kb >= ts)

    @pl.when(jnp.max(n_ge) > topk)
    def _break_ties():
        r = topk - count(lambda kb, off: kb > ts)

        def bit_body(t, p):
            cand = p | jnp.left_shift(jnp.int32(1), (seq.bit_length() - 1) - t)
            c = count(lambda kb, off: (kb == ts) & (off + kiota < cand))
            return jnp.where(c < r, cand, p)
        p_last = lax.fori_loop(0, seq.bit_length(), bit_body, jnp.zeros((1, TQ), I32))

        def demote(kt, carry):
            off = pl.multiple_of(kt * TK, TK)
            kb = keys_ref[pl.ds(off, TK), :]
            keys_ref[pl.ds(off, TK), :] = jnp.where((kb == ts) & (off + kiota > p_last), ts - 1, kb)
            return carry
        lax.fori_loop(0, nkt, demote, 0)

    def out_body(kt, carry):
        off = pl.multiple_of(kt * TK, TK)

        @pl.when(kt < nkt)
        def _():
            kb = keys_ref[pl.ds(off, TK), :]
            sel = (kb >= ts) & (off + kiota < limq)
            mask_ref[0, 0, pl.ds(off, TK), :] = jnp.where(sel, 1, 0).astype(mask_ref.dtype)

        @pl.when(kt >= nkt)
        def _():
            mask_ref[0, 0, pl.ds(off, TK), :] = jnp.zeros((TK, TQ), mask_ref.dtype)
        return carry

    lax.fori_loop(0, seq // TK, out_body, 0)


def dsa_select(qi, ki, wi, topk):
    bsz, _, seq, _ = qi.shape
    nq = seq // TQ
    return pl.pallas_call(
        functools.partial(_dsa_select_kernel, seq=seq, topk=topk),
        out_shape=jax.ShapeDtypeStruct((bsz, nq, seq, TQ), jnp.int8),
        grid=(bsz, nq),
        in_specs=[pl.BlockSpec((1, IDX_PAIRS, TQ, LANES), lambda b, j: (b, 0, j, 0)),
                  pl.BlockSpec((1, seq, LANES), lambda b, j: (b, 0, 0)),
                  pl.BlockSpec((1, TQ, LANES), lambda b, j: (b, j, 0))],
        out_specs=pl.BlockSpec((1, 1, seq, TQ), lambda b, j: (b, j, 0, 0)),
        scratch_shapes=[pltpu.VMEM((seq, TQ), I32)],
        compiler_params=_cp("parallel", "parallel"),
    )(qi, ki, wi)


VT_ROWS = LANES + 16
S_CHUNK = 64


def _dsa_attn_kernel(jmap_ref, ktmap_ref, q_ref, k_ref, vt_ref, mask_ref, o_ref,
                     qm_ref, m_ref, l_ref, acc_ref, bias_ref, s_ref, p_ref):
    step = pl.program_id(1)
    j = jmap_ref[step]
    kt = ktmap_ref[step]
    last = (j * TQ + TQ - 1) // TK

    @pl.when(kt == 0)
    def _init():
        for p in range(PAIRS):
            qm_ref[p] = _pair_split(q_ref[0, p])
        m_ref[...] = jnp.full(m_ref.shape, -jnp.inf, F32)
        l_ref[...] = jnp.zeros(l_ref.shape, F32)
        acc_ref[...] = jnp.zeros(acc_ref.shape, F32)

    bias = (mask_ref[0, 0].astype(F32) - 1.0) * 1e30
    bias_ref[:, :TQ] = bias
    bias_ref[:, TQ:] = bias

    def scores(p):
        s = _nt(k_ref[0, p], qm_ref[p]) + bias_ref[...]
        s_ref[p % 2] = s
        return jnp.max(s, axis=0, keepdims=True)

    m_tile = scores(0)
    for p in range(PAIRS):
        slot = p % 2
        m_old = m_ref[p]
        m_new = jnp.maximum(m_old, m_tile)
        alpha = jnp.exp2(m_old - m_new)
        m_ref[p] = m_new
        if p + 1 < PAIRS:
            m_tile = scores(p + 1)
        for c in range(TK // S_CHUNK):
            rows = slice(c * S_CHUNK, (c + 1) * S_CHUNK)
            p_ref[slot, rows, :] = jnp.exp2(s_ref[slot, rows, :] - m_new).astype(BF16)
        pv = _dot(vt_ref[0, p], p_ref[slot])
        acc_ref[p] = acc_ref[p] * alpha + pv[:LANES]
        l_ref[p] = l_ref[p] * alpha + pv[LANES:LANES + 1]

    @pl.when(kt == last)
    def _finish():
        for p in range(PAIRS):
            a = acc_ref[p] / l_ref[p]
            ot = jnp.concatenate([a[0:HEAD, 0:TQ], a[HEAD:2 * HEAD, TQ:2 * TQ]], axis=0)
            o_ref[0, :, p * LANES:(p + 1) * LANES] = ot.T.astype(o_ref.dtype)


def dsa_attn(q, k, vt, mask):
    bsz, _, seq, _ = q.shape
    nq = seq // TQ
    visits = [(j, t) for j in range(nq) for t in range((j * TQ + TQ - 1) // TK + 1)]
    jmap = jnp.asarray([jt[0] for jt in visits], I32)
    ktmap = jnp.asarray([jt[1] for jt in visits], I32)
    grid_spec = pltpu.PrefetchScalarGridSpec(
        num_scalar_prefetch=2,
        grid=(bsz, len(visits)),
        in_specs=[pl.BlockSpec((1, PAIRS, TQ, LANES), lambda b, s, jm, km: (b, 0, jm[s], 0)),
                  pl.BlockSpec((1, PAIRS, TK, LANES), lambda b, s, jm, km: (b, 0, km[s], 0)),
                  pl.BlockSpec((1, PAIRS, VT_ROWS, TK), lambda b, s, jm, km: (b, 0, 0, km[s])),
                  pl.BlockSpec((1, 1, TK, TQ), lambda b, s, jm, km: (b, jm[s], km[s], 0))],
        out_specs=pl.BlockSpec((1, TQ, PAIRS * LANES), lambda b, s, jm, km: (b, jm[s], 0)),
        scratch_shapes=[pltpu.VMEM((PAIRS, 2 * TQ, LANES), BF16),
                        pltpu.VMEM((PAIRS, 1, 2 * TQ), F32),
                        pltpu.VMEM((PAIRS, 1, 2 * TQ), F32),
                        pltpu.VMEM((PAIRS, LANES, 2 * TQ), F32),
                        pltpu.VMEM((TK, 2 * TQ), F32),
                        pltpu.VMEM((2, TK, 2 * TQ), F32),
                        pltpu.VMEM((2, TK, 2 * TQ), BF16)])
    return pl.pallas_call(
        _dsa_attn_kernel,
        out_shape=jax.ShapeDtypeStruct((bsz, seq, PAIRS * LANES), BF16),
        grid_spec=grid_spec,
        compiler_params=_cp("parallel", "arbitrary"),
    )(jmap, ktmap, q, k, vt, mask)


def dsa_mixer(h, bsz, seq, w_in, q_gain, k_gain, kidx_gain):
    assert seq % TK == 0 and TK % TQ == 0
    topk = min(TOPK_MAX, seq // 4)
    q, k, vt, qi, ki, wi = dsa_proj(h, bsz, seq, w_in, q_gain, k_gain, kidx_gain)
    mask = dsa_select(qi, ki, wi, topk)
    return dsa_attn(q, k, vt, mask).reshape(bsz * seq, PAIRS * LANES)


RW_PREV = 8
RW_C = 64
RW_CH = 8


def _rwkv_proj_kernel(x_ref, xp_ref, gn_ref, mu_ref, wr_ref, wk_ref, wv_ref, w1_ref, w2_ref, a1_ref, a2_ref,
                      g1_ref, g2_ref, w0_ref, a0_ref, kk_ref, ka_ref,
                      r_out, lw_out, k_out, v_out, kk_out, b_out, g_out, *, tiles_per_seq):
    i = pl.program_id(0)
    gn = gn_ref[...]
    h = _rms(x_ref[...], gn)
    hp = _rms(xp_ref[...], gn)
    hp_last = jnp.where((i % tiles_per_seq) == 0, 0.0, hp[RW_PREV - 1:RW_PREV, :])
    rows = lax.broadcasted_iota(I32, h.shape, 0)
    dh = jnp.where(rows == 0, hp_last, pltpu.roll(h, 1, 0)) - h
    mu = mu_ref[...]
    xs = lambda n: (h + dh * mu[n:n + 1, :]).astype(BF16)
    r = _dot(xs(0), wr_ref[...])
    k = _dot(xs(1), wk_ref[...])
    v_out[...] = _dot(xs(2), wv_ref[...])
    wl = w0_ref[...] + _dot(jnp.tanh(_dot(xs(3), w1_ref[...])).astype(BF16), w2_ref[...])
    z = -wl
    w_log = -(jnp.maximum(z, 0.0) + jnp.log(1.0 + jnp.exp(-jnp.abs(z)))) - 0.5
    lw_out[...] = -jnp.exp(w_log)
    a = jax.nn.sigmoid(a0_ref[...] + _dot(_dot(xs(4), a1_ref[...]).astype(BF16), a2_ref[...]))
    g_out[...] = _dot(jax.nn.sigmoid(_dot(xs(5), g1_ref[...])).astype(BF16), g2_ref[...])
    r_out[...] = r
    k_out[...] = k * (1.0 + (a - 1.0) * ka_ref[...])
    kk = k * kk_ref[...]
    ones = _seg64_ones()
    for p in range(kk.shape[1] // SEG_W):
        sl = slice(p * SEG_W, (p + 1) * SEG_W)
        kp = kk[:, sl]
        kn = kp / jnp.maximum(jnp.sqrt(_seg64_sum(kp * kp, ones)), 1e-12)
        kk_out[:, sl] = kn
        b_out[:, sl] = kn * a[:, sl]


def rwkv_proj(x2, seq, gn, mu, w_rkv, w0, w1, w2, a0, a1, a2, g1, g2, k_k, k_a, tm=256):
    t, d = x2.shape
    pad_c = lambda w: jnp.pad(w, ((0, 0), (0, LANES - w.shape[1]))).astype(BF16)
    pad_r = lambda w: jnp.pad(w, ((0, LANES - w.shape[0]), (0, 0))).astype(BF16)
    row = lambda i: (i, 0)
    fixed = lambda i: (0, 0)
    vec = lambda a: a.reshape(1, d)
    full = lambda a: pl.BlockSpec(a.shape, fixed)
    args = [x2, x2, vec(gn), jnp.pad(mu, ((0, 2), (0, 0))),
            w_rkv[0].astype(BF16), w_rkv[1].astype(BF16), w_rkv[2].astype(BF16),
            pad_c(w1), pad_r(w2), pad_c(a1), pad_r(a2), g1.astype(BF16), g2.astype(BF16),
            vec(w0), vec(a0), vec(k_k), vec(k_a)]
    in_specs = [pl.BlockSpec((tm, d), row),
                pl.BlockSpec((RW_PREV, d), lambda i: (jnp.maximum(i * (tm // RW_PREV) - 1, 0), 0))]
    in_specs += [full(a) for a in args[2:]]
    return pl.pallas_call(
        functools.partial(_rwkv_proj_kernel, tiles_per_seq=seq // tm),
        out_shape=tuple(jax.ShapeDtypeStruct((t, d), F32) for _ in range(7)),
        grid=(t // tm,),
        in_specs=in_specs,
        out_specs=tuple(pl.BlockSpec((tm, d), row) for _ in range(7)),
        compiler_params=_cp("parallel"),
    )(*args)


def _bdot(a, b):
    return _dot(a.astype(BF16), b.astype(BF16))


def _rwkv_chunk_terms(r, lw, k2, v, kk, b, tick):
    c = RW_C
    cat = jnp.concatenate
    each = lambda f, *ls: [f(*a) for a in zip(*ls)]
    row = lax.broadcasted_iota(I32, (2 * c, 2 * c), 0)
    col = lax.broadcasted_iota(I32, (2 * c, 2 * c), 1)
    r_c = lax.broadcasted_iota(I32, (c, c), 0)
    c_c = lax.broadcasted_iota(I32, (c, c), 1)
    tri = jnp.where(c_c <= r_c, 1.0, 0.0).astype(BF16)

    def cum(lw_):
        hi = lw_.astype(BF16)
        rem = lw_ - hi.astype(F32)
        mid = rem.astype(BF16)
        lo = (rem - mid.astype(F32)).astype(BF16)
        return _dot(tri, hi) + _dot(tri, mid) + _dot(tri, lo)
    cs = each(cum, lw)
    e_pos = each(jnp.exp, cs)
    e_neg = each(lambda x: jnp.exp(-x), cs)
    mul = lambda x, y: x * y
    rt = each(mul, r, e_pos)
    kt = each(mul, k2, e_neg)
    bt = each(mul, b, e_neg)
    kkt = each(lambda x, s, l: x * jnp.exp(s - l), kk, cs, lw)
    g_end = each(lambda e: e[c - 1:c, :], e_pos)
    khat = each(mul, kt, g_end)
    bhat = each(mul, bt, g_end)
    lane = lax.broadcasted_iota(I32, (c, LANES), 1)
    t_row = lax.broadcasted_iota(I32, (c, LANES), 0)
    m0 = lane < HEAD
    h0 = lambda x: jnp.where(m0, x, 0.0)
    h1 = lambda x: jnp.where(m0, 0.0, x)
    g0 = each(lambda kq, rr, bb, kk_: _nt(cat([h0(kq), h0(rr)]).astype(BF16), cat([bb, kk_]).astype(BF16)),
              kkt, rt, bt, kt)
    tick()
    g1 = each(lambda kq, rr, bb, kk_: _nt(cat([h1(rr), h1(kq)]).astype(BF16), cat([kk_, bb]).astype(BF16)),
              kkt, rt, bt, kt)
    top, left = row < c, col < c
    m_ab0, m_ab1 = top & left & (col < row), ~top & ~left & (col < row)
    m_ak0, m_ak1 = top & ~left & (col - c < row), ~top & left & (col < row - c)
    a_b = each(lambda x0, x1: jnp.where(m_ab0, x0, 0.0) + jnp.where(m_ab1, x1, 0.0), g0, g1)
    a_k = each(lambda x0, x1: jnp.where(m_ak0, x0, 0.0) + jnp.where(m_ak1, x1, 0.0), g0, g1)
    eye = jnp.where(row == col, 1.0, 0.0)
    x = each(lambda a: eye - a, a_b)
    pw = each(lambda a: _bdot(a, a), a_b)
    tick()
    for it in range(5):
        x = each(lambda xx, pp: xx + _bdot(xx, pp), x, pw)
        tick()
        if it < 4:
            pw = each(lambda pp: _bdot(pp, pp), pw)
    v0, v1 = each(h0, v), each(h1, v)
    akv = each(lambda a, va, vb: _bdot(a, cat([vb, va])), a_k, v0, v1)
    tick()
    wu = each(lambda xx, kq, av: _bdot(xx, cat([cat([h0(kq), h1(kq)]), av], axis=1)), x, kkt, akv)
    incl = jnp.where(m0, lane, lane - HEAD) <= t_row
    zeros = jnp.zeros((c, LANES), F32)
    rhs0 = each(lambda w_, va: cat([-w_[0:c], cat([zeros, va], axis=1)]), wu, v0)
    rhs1 = each(lambda w_, vb: cat([cat([zeros, vb], axis=1), -w_[c:2 * c]]), wu, v1)
    o0 = each(lambda g, rh: _bdot(jnp.where(incl, g[c:2 * c], 0.0), rh), g0, rhs0)
    o1 = each(lambda g, rh: _bdot(jnp.where(incl, g[0:c], 0.0), rh), g1, rhs1)
    rq = each(lambda rr, a0, a1: rr + a0[:, :LANES] + a1[:, :LANES], rt, o0, o1)
    yin = each(lambda a0, a1: a0[:, LANES:] + a1[:, LANES:], o0, o1)
    pd = each(lambda bh, kh, ra, rb: _tn(cat([h0(bh), h0(kh), h1(kh), h1(bh)]).astype(BF16),
                                         cat([ra, rb]).astype(BF16)), bhat, khat, rhs0, rhs1)
    phi = each(lambda p_, ge: p_[:, :LANES] + jnp.where(row == col, jnp.broadcast_to(ge, (2 * c, LANES)), 0.0),
               pd, g_end)
    return [(a, b_, c_, p_[:, LANES:]) for a, b_, c_, p_ in zip(rq, yin, phi, pd)]


def _rwkv_scan_kernel(r_ref, lw_ref, k_ref, v_ref, kk_ref, b_ref, y_ref, h_ref, lhs_ref, yin_ref, dh_ref):
    s = pl.program_id(2)
    cur = s % 2
    prev = 1 - cur
    rows = [slice(ci * RW_C, (ci + 1) * RW_C) for ci in range(RW_CH)]

    @pl.when(s == 0)
    def _():
        lhs_ref[prev] = jnp.zeros(lhs_ref.shape[1:], lhs_ref.dtype)
        yin_ref[prev] = jnp.zeros(yin_ref.shape[1:], F32)
        dh_ref[prev] = jnp.zeros(dh_ref.shape[1:], F32)
        h_ref[...] = jnp.zeros(h_ref.shape, F32)

    state = [jnp.where(s <= 1, 0.0, h_ref[...])]
    links = iter(range(RW_CH))

    def recurrence_step():
        ci = next(links, None)
        if ci is None:
            return
        both = _dot(lhs_ref[prev, ci], state[0].astype(BF16))
        y_ref[0, rows[ci], :] = both[0:RW_C] + yin_ref[prev, ci]
        state[0] = both[RW_C:] + dh_ref[prev, ci]

    ld = lambda ref: [ref[0, sl, :] for sl in rows]
    terms = _rwkv_chunk_terms(ld(r_ref), ld(lw_ref), ld(k_ref), ld(v_ref), ld(kk_ref), ld(b_ref), recurrence_step)
    for _ in range(RW_CH):
        recurrence_step()
    h_ref[...] = state[0]
    for ci, (rq, yin, phi, dh) in enumerate(terms):
        lhs_ref[cur, ci] = jnp.concatenate([rq, phi]).astype(BF16)
        yin_ref[cur, ci] = yin
        dh_ref[cur, ci] = dh


def rwkv_scan(r, lw, k2, v, kk, b, bsz, seq):
    d = r.shape[-1]
    rows = RW_C * RW_CH
    nblk = seq // rows
    in_spec = pl.BlockSpec((1, rows, LANES), lambda bb, p, s: (bb, jnp.minimum(s, nblk - 1), p))
    out_spec = pl.BlockSpec((1, rows, LANES), lambda bb, p, s: (bb, jnp.maximum(s - 1, 0), p))
    shp = lambda a: a.reshape(bsz, seq, d)
    return pl.pallas_call(
        _rwkv_scan_kernel,
        out_shape=jax.ShapeDtypeStruct((bsz, seq, d), F32),
        grid=(bsz, d // LANES, nblk + 1),
        in_specs=[in_spec] * 6,
        out_specs=out_spec,
        scratch_shapes=[pltpu.VMEM((LANES, LANES), F32),
                        pltpu.VMEM((2, RW_CH, RW_C + LANES, LANES), BF16),
                        pltpu.VMEM((2, RW_CH, RW_C, LANES), F32),
                        pltpu.VMEM((2, RW_CH, LANES, LANES), F32)],
        compiler_params=_cp("parallel", "parallel", "arbitrary"),
    )(shp(r), shp(lw), shp(k2), shp(v), shp(kk), shp(b)).reshape(bsz * seq, d)


def _rwkv_post_kernel(y_ref, r_ref, k_ref, v_ref, g_ref, lng_ref, lnb_ref, rk_ref, w_ref, x_ref, gn_ref,
                      xo_ref, ho_ref, o_scr):
    ones = _seg64_ones()
    for p in range(y_ref.shape[1] // SEG_W):
        sl = slice(p * SEG_W, (p + 1) * SEG_W)
        y = y_ref[:, sl]
        dv = y - _seg64_sum(y, ones) * (1.0 / HEAD)
        var = _seg64_sum(dv * dv, ones) * (1.0 / HEAD)
        yn = dv * lax.rsqrt(var + GN_EPS) * lng_ref[:, sl] + lnb_ref[:, sl]
        bonus = _seg64_sum(r_ref[:, sl] * k_ref[:, sl] * rk_ref[:, sl], ones) * v_ref[:, sl]
        o_scr[:, sl] = ((yn + bonus) * g_ref[:, sl]).astype(o_scr.dtype)
    xn = x_ref[...] + _dot(o_scr[...], w_ref[...])
    xo_ref[...] = xn
    ho_ref[...] = _rms(xn, gn_ref[...]).astype(ho_ref.dtype)


def rwkv_post(y, r, k2, v, g, ln_g, ln_b, r_k, w_out, x2, gn, tm=256):
    t, d = x2.shape
    row = lambda i: (i, 0)
    fixed = lambda i: (0, 0)
    tile = pl.BlockSpec((tm, d), row)
    vecs = pl.BlockSpec((1, d), fixed)
    return pl.pallas_call(
        _rwkv_post_kernel,
        out_shape=(jax.ShapeDtypeStruct((t, d), F32), jax.ShapeDtypeStruct((t, d), BF16)),
        grid=(t // tm,),
        in_specs=[tile] * 5 + [vecs] * 3 + [pl.BlockSpec((d, d), fixed), tile, vecs],
        out_specs=(tile, tile),
        scratch_shapes=[pltpu.VMEM((tm, d), BF16)],
        compiler_params=_cp("parallel"),
    )(y, r, k2, v, g, ln_g.reshape(1, d), ln_b.reshape(1, d), r_k.reshape(1, d), w_out.astype(BF16), x2,
      gn.reshape(1, d))


S5_L = 16


def _split(a):
    hi = a.astype(BF16)
    return hi, (a - hi.astype(F32)).astype(BF16)


def _dot3(a, b_hi, b_lo):
    a_hi, a_lo = _split(a)
    return _dot(a_hi, b_hi) + _dot(a_lo, b_hi) + _dot(a_hi, b_lo)


def _s5_local_kernel(u_ref, grh_ref, grl_ref, gih_ref, gil_ref, xr_ref, xi_ref):
    u0, u1 = u_ref[0], u_ref[1]
    xr_ref[...] = _dot3(u0, grh_ref[0, 0], grl_ref[0, 0]) + _dot3(u1, grh_ref[0, 1], grl_ref[0, 1])
    xi_ref[...] = _dot3(u0, gih_ref[0, 0], gil_ref[0, 0]) + _dot3(u1, gih_ref[0, 1], gil_ref[0, 1])


def _s5_carry_kernel(er_ref, ei_ref, lr_ref, li_ref, pr_ref, pi_ref):
    lr = lr_ref[...]
    li = li_ref[...]

    def body(n, st):
        sr, si = st
        pr_ref[n] = sr
        pi_ref[n] = si
        return (lr * sr - li * si + er_ref[n], lr * si + li * sr + ei_ref[n])
    zero = jnp.zeros(lr.shape, F32)
    lax.fori_loop(0, er_ref.shape[0], body, (zero, zero))


def _s5_out_kernel(u_ref, pr_ref, pi_ref, kh_ref, kl_ref, erh_ref, erl_ref, eih_ref, eil_ref, y_ref):
    pr = pr_ref[...]
    pi = pi_ref[...]
    for i in range(2):
        y_ref[i] = (_dot3(u_ref[i], kh_ref[i], kl_ref[i]) + _dot3(pr, erh_ref[0, i], erl_ref[0, i])
                    + _dot3(pi, eih_ref[0, i], eil_ref[0, i]))


def s5_ssm(h3, a_re, a_im, log_step, b_re, b_im, c_re, c_im):
    bsz, seq, d = h3.shape
    ng, ns = a_re.shape
    gc = d // ng
    L = S5_L
    nc = seq // L
    n = bsz * nc
    step = jnp.exp(log_step.astype(F32))[:, None]
    lam = lax.complex(a_re.astype(F32), a_im.astype(F32))
    lam_bar = jnp.exp(lam * step)
    b_bar = ((lam_bar - 1.0) / lam)[..., None] * lax.complex(b_re.astype(F32), b_im.astype(F32))
    cc = lax.complex(c_re.astype(F32), c_im.astype(F32))
    pw = jnp.exp((lam * step)[:, None, :] * jnp.arange(L + 1, dtype=F32)[None, :, None])
    lag = jnp.arange(L)[None, :] - jnp.arange(L)[:, None]
    kfull = jnp.einsum('gcp,gstp,gpe->gsetc', cc, pw[:, jnp.clip(lag, 0, L)], b_bar)
    kmat = jnp.where((lag >= 0)[None, :, None, :, None], jnp.real(kfull), 0.0).reshape(ng, L * gc, L * gc)
    gfull = jnp.einsum('gsp,gpe->gsep', pw[:, L - 1 - jnp.arange(L)], b_bar).reshape(ng, L * gc, ns)
    efull = jnp.einsum('gcp,gtp->gptc', cc, pw[:, 1:]).reshape(ng, ns, L * gc)
    lam_l = pw[:, L]
    assert 2 * ns == LANES and ng % 16 == 0
    nq = ng // 2

    def cols(m):
        m4 = m.reshape(nq, 2, m.shape[1], ns)
        z = jnp.zeros_like(m4[:, 0])
        return jnp.stack([jnp.concatenate([m4[:, 0], z], -1), jnp.concatenate([z, m4[:, 1]], -1)], 1)

    def rows(m):
        m4 = m.reshape(nq, 2, ns, m.shape[2])
        z = jnp.zeros_like(m4[:, 0])
        return jnp.stack([jnp.concatenate([m4[:, 0], z], -2), jnp.concatenate([z, m4[:, 1]], -2)], 1)

    kh, kl = _split(kmat)
    grh, grl = _split(cols(jnp.real(gfull)))
    gih, gil = _split(cols(jnp.imag(gfull)))
    erh, erl = _split(rows(jnp.real(efull)))
    eih, eil = _split(rows(-jnp.imag(efull)))
    lr = jnp.real(lam_l).reshape(nq, LANES)
    li = jnp.imag(lam_l).reshape(nq, LANES)

    lw = L * gc
    u = h3.reshape(bsz, nc, L, ng, gc).transpose(3, 0, 1, 2, 4).reshape(ng, n, lw)
    pair3 = lambda q: (q, 0, 0)
    pair4 = lambda q: (q, 0, 0, 0)
    col = lambda q: (0, q)
    gspec = pl.BlockSpec((1, 2, lw, LANES), pair4)
    xr, xi = pl.pallas_call(
        _s5_local_kernel,
        out_shape=(jax.ShapeDtypeStruct((n, nq * LANES), F32),) * 2,
        grid=(nq,),
        in_specs=[pl.BlockSpec((2, n, lw), pair3), gspec, gspec, gspec, gspec],
        out_specs=(pl.BlockSpec((n, LANES), col),) * 2,
        compiler_params=_cp("parallel"),
    )(u, grh, grl, gih, gil)
    st_spec = pl.BlockSpec((nc, 8, LANES), lambda b, j: (b, j, 0))
    lam_spec = pl.BlockSpec((8, LANES), lambda b, j: (j, 0))
    pr, pi = pl.pallas_call(
        _s5_carry_kernel,
        out_shape=(jax.ShapeDtypeStruct((n, nq, LANES), F32),) * 2,
        grid=(bsz, nq // 8),
        in_specs=[st_spec, st_spec, lam_spec, lam_spec],
        out_specs=(st_spec, st_spec),
        compiler_params=_cp("parallel", "parallel"),
    )(xr.reshape(n, nq, LANES), xi.reshape(n, nq, LANES), lr, li)
    espec = pl.BlockSpec((1, 2, LANES, lw), pair4)
    kspec = pl.BlockSpec((2, lw, lw), pair3)
    y = pl.pallas_call(
        _s5_out_kernel,
        out_shape=jax.ShapeDtypeStruct((ng, n, lw), F32),
        grid=(nq,),
        in_specs=[pl.BlockSpec((2, n, lw), pair3), pl.BlockSpec((n, LANES), col), pl.BlockSpec((n, LANES), col),
                  kspec, kspec, espec, espec, espec, espec],
        out_specs=pl.BlockSpec((2, n, lw), pair3),
        compiler_params=_cp("parallel"),
    )(u, pr.reshape(n, nq * LANES), pi.reshape(n, nq * LANES), kh, kl, erh, erl, eih, eil)
    return y.reshape(ng, bsz, nc, L, gc).transpose(1, 2, 3, 0, 4).reshape(bsz * seq, d)


def _s5_glu_kernel(ys_ref, h_ref, d_ref, w_ref, b_ref, x_ref, gn_ref, xo_ref, ho_ref):
    y = ys_ref[...] + d_ref[...] * h_ref[...]
    gelu = 0.5 * y * (1.0 + jnp.tanh(math.sqrt(2.0 / math.pi) * (y + 0.044715 * (y * y * y))))
    z = _dot(gelu.astype(BF16), w_ref[...]) + b_ref[...]
    dm = x_ref.shape[1]
    xn = x_ref[...] + z[:, :dm] * jax.nn.sigmoid(z[:, dm:])
    xo_ref[...] = xn
    ho_ref[...] = _rms(xn, gn_ref[...]).astype(ho_ref.dtype)


def s5_glu(ys, h, d_skip, w_glu, b_glu, x2, gn, tm=256):
    t, d = x2.shape
    row = lambda i: (i, 0)
    fixed = lambda i: (0, 0)
    tile = pl.BlockSpec((tm, d), row)
    return pl.pallas_call(
        _s5_glu_kernel,
        out_shape=(jax.ShapeDtypeStruct((t, d), F32), jax.ShapeDtypeStruct((t, d), BF16)),
        grid=(t // tm,),
        in_specs=[tile, tile, pl.BlockSpec((1, d), fixed), pl.BlockSpec((d, 2 * d), fixed),
                  pl.BlockSpec((1, 2 * d), fixed), tile, pl.BlockSpec((1, d), fixed)],
        out_specs=(tile, tile),
        compiler_params=_cp("parallel"),
    )(ys, h, d_skip.reshape(1, d), w_glu.astype(BF16), b_glu.reshape(1, 2 * d), x2, gn.reshape(1, d))


def kernel(x, norm_mix, norm_ffn, dsa_w_in, dsa_q_norm, dsa_k_norm, dsa_kidx_norm, dsa_w_out, rwkv_mu, rwkv_w_rkv, rwkv_w0, rwkv_w1, rwkv_w2, rwkv_a0, rwkv_a1, rwkv_a2, rwkv_g1, rwkv_g2, rwkv_k_k, rwkv_k_a, rwkv_r_k, rwkv_ln_g, rwkv_ln_b, rwkv_w_out, s5_a_re, s5_a_im, s5_log_step, s5_b_re, s5_b_im, s5_c_re, s5_c_im, s5_d, s5_w_glu, s5_b_glu, ffn_w_up, ffn_conv_w, ffn_conv_b, ffn_w_down):
    bsz, seq, d = x.shape
    depth = norm_mix.shape[0]
    x2 = x.reshape(bsz * seq, d)
    for i in range(depth):
        kind, j = i % 3, i // 3
        if kind == 0:
            h = rmsnorm(x2, norm_mix[i], BF16)
            o = dsa_mixer(h, bsz, seq, dsa_w_in[j], dsa_q_norm[j], dsa_k_norm[j], dsa_kidx_norm[j])
            x2, h = mm_res_norm(o, dsa_w_out[j].astype(BF16), x2, norm_ffn[i], BF16)
        elif kind == 1:
            r, lw, k2, v, kk, b, g = rwkv_proj(x2, seq, norm_mix[i], rwkv_mu[j], rwkv_w_rkv[j], rwkv_w0[j],
                                               rwkv_w1[j], rwkv_w2[j], rwkv_a0[j], rwkv_a1[j], rwkv_a2[j],
                                               rwkv_g1[j], rwkv_g2[j], rwkv_k_k[j], rwkv_k_a[j])
            y = rwkv_scan(r, lw, k2, v, kk, b, bsz, seq)
            x2, h = rwkv_post(y, r, k2, v, g, rwkv_ln_g[j], rwkv_ln_b[j], rwkv_r_k[j].reshape(d), rwkv_w_out[j],
                              x2, norm_ffn[i])
        else:
            hf = rmsnorm(x2, norm_mix[i], F32)
            ys = s5_ssm(hf.reshape(bsz, seq, d), s5_a_re[j], s5_a_im[j], s5_log_step[j], s5_b_re[j], s5_b_im[j],
                        s5_c_re[j], s5_c_im[j])
            x2, h = s5_glu(ys, hf, s5_d[j], s5_w_glu[j], s5_b_glu[j], x2, norm_ffn[i])
        act = ffn_up(h, ffn_w_up[i].astype(BF16), ffn_conv_w[i], ffn_conv_b[i], seq)
        x2, _ = mm_res_norm(act, ffn_w_down[i].astype(BF16), x2, None, None)
    return x2.reshape(bsz, seq, d)
```

```python
import functools
import math

import jax
import jax.numpy as jnp
from jax import lax
from jax.experimental import pallas as pl
from jax.experimental.pallas import tpu as pltpu

F32 = jnp.float32
BF16 = jnp.bfloat16
I32 = jnp.int32

EPS = 1e-6
NEG_INF = -1e30
LANES = 128
HEAD = 64
CHUNK = 64
TOPK_MAX = 256
GN_EPS = 64e-5
INT_MIN = -(2 ** 31)

VMEM_LIMIT = 56 * 1024 * 1024


def _cp(*sem):
    return pltpu.CompilerParams(dimension_semantics=sem, vmem_limit_bytes=VMEM_LIMIT)


def _nt(a, b):
    return lax.dot_general(a, b, (((1,), (1,)), ((), ())), preferred_element_type=F32)


def _tn(a, b):
    return lax.dot_general(a, b, (((0,), (0,)), ((), ())), preferred_element_type=F32)


def _dot(a, b):
    return jnp.dot(a, b, preferred_element_type=F32)


def _rms(x, g):
    ms = jnp.mean(x * x, axis=-1, keepdims=True)
    return x * lax.rsqrt(ms + EPS) * g


SEG_W = 2 * LANES


def _seg64_ones():
    r = lax.broadcasted_iota(I32, (SEG_W, SEG_W), 0) // HEAD
    c = lax.broadcasted_iota(I32, (SEG_W, SEG_W), 1) // HEAD
    return jnp.where(r == c, 1.0, 0.0).astype(BF16)


def _seg64_sum(x, ones):
    hi = x.astype(BF16)
    lo = (x - hi.astype(F32)).astype(BF16)
    return _dot(hi, ones) + _dot(lo, ones)


def _norm_kernel(x_ref, g_ref, o_ref):
    o_ref[...] = _rms(x_ref[...], g_ref[...]).astype(o_ref.dtype)


def rmsnorm(x2, g, out_dtype, tm=1024):
    t, d = x2.shape
    return pl.pallas_call(
        _norm_kernel,
        out_shape=jax.ShapeDtypeStruct((t, d), out_dtype),
        grid=(t // tm,),
        in_specs=[pl.BlockSpec((tm, d), lambda i: (i, 0)),
                  pl.BlockSpec((1, d), lambda i: (0, 0))],
        out_specs=pl.BlockSpec((tm, d), lambda i: (i, 0)),
        compiler_params=_cp("parallel"),
    )(x2, g.reshape(1, d))


def _mm_res_norm_kernel(a_ref, w_ref, x_ref, g_ref, xo_ref, ho_ref):
    xn = x_ref[...] + _dot(a_ref[...], w_ref[...])
    xo_ref[...] = xn
    ho_ref[...] = _rms(xn, g_ref[...]).astype(ho_ref.dtype)


def _mm_res_kernel(a_ref, w_ref, x_ref, xo_ref):
    xo_ref[...] = x_ref[...] + _dot(a_ref[...], w_ref[...])


def mm_res_norm(a, w, x2, g, h_dtype, tm=512):
    t, k = a.shape
    d = w.shape[1]
    row = lambda i: (i, 0)
    fixed = lambda i: (0, 0)
    in_specs = [pl.BlockSpec((tm, k), row), pl.BlockSpec((k, d), fixed), pl.BlockSpec((tm, d), row)]
    if g is None:
        return pl.pallas_call(
            _mm_res_kernel,
            out_shape=jax.ShapeDtypeStruct((t, d), F32),
            grid=(t // tm,), in_specs=in_specs, out_specs=pl.BlockSpec((tm, d), row),
            compiler_params=_cp("parallel"),
        )(a, w, x2), None
    return pl.pallas_call(
        _mm_res_norm_kernel,
        out_shape=(jax.ShapeDtypeStruct((t, d), F32), jax.ShapeDtypeStruct((t, d), h_dtype)),
        grid=(t // tm,),
        in_specs=in_specs + [pl.BlockSpec((1, d), fixed)],
        out_specs=(pl.BlockSpec((tm, d), row), pl.BlockSpec((tm, d), row)),
        compiler_params=_cp("parallel"),
    )(a, w, x2, g.reshape(1, d))


PREV_ROWS = 16


def _ffn_up_kernel(h_ref, hp_ref, wg_ref, wv_ref, cwg_ref, cwv_ref, cbg_ref, cbv_ref, o_ref, *, tiles_per_seq):
    i = pl.program_id(1)
    h = h_ref[...]
    hp = hp_ref[...]
    seq_start = (i % tiles_per_seq) == 0

    def branch(w_ref, cw_ref, cb_ref):
        w = w_ref[...]
        u = _dot(h, w)
        up = jnp.where(seq_start, 0.0, _dot(hp, w))
        rows = lax.broadcasted_iota(I32, u.shape, 0)
        last = up[PREV_ROWS - 1:PREV_ROWS, :]
        u1 = jnp.where(rows == 0, last, pltpu.roll(u, 1, 0))
        u2 = jnp.where(rows == 0, up[PREV_ROWS - 2:PREV_ROWS - 1, :],
                       jnp.where(rows == 1, last, pltpu.roll(u, 2, 0)))
        cw = cw_ref[...]
        return cw[0:1, :] * u2 + cw[1:2, :] * u1 + cw[2:3, :] * u + cb_ref[...]

    gate = branch(wg_ref, cwg_ref, cbg_ref)
    val = branch(wv_ref, cwv_ref, cbv_ref)
    o_ref[...] = (gate * jax.nn.sigmoid(gate) * val).astype(o_ref.dtype)


def ffn_up(h, w_up, conv_w, conv_b, seq, tm=512):
    t, d = h.shape
    f = w_up.shape[1] // 2
    tn = f // 2
    assert tn % LANES == 0 and seq % tm == 0
    nj = f // tn
    cw = jnp.zeros((8, 2 * f), F32).at[:conv_w.shape[0]].set(conv_w)
    cb = conv_b.reshape(1, 2 * f)
    prev = lambda j, i: (jnp.maximum(i * (tm // PREV_ROWS) - 1, 0), 0)
    return pl.pallas_call(
        functools.partial(_ffn_up_kernel, tiles_per_seq=seq // tm),
        out_shape=jax.ShapeDtypeStruct((t, f), BF16),
        grid=(nj, t // tm),
        in_specs=[pl.BlockSpec((tm, d), lambda j, i: (i, 0)),
                  pl.BlockSpec((PREV_ROWS, d), prev),
                  pl.BlockSpec((d, tn), lambda j, i: (0, j)),
                  pl.BlockSpec((d, tn), lambda j, i: (0, j + nj)),
                  pl.BlockSpec((8, tn), lambda j, i: (0, j)),
                  pl.BlockSpec((8, tn), lambda j, i: (0, j + nj)),
                  pl.BlockSpec((1, tn), lambda j, i: (0, j)),
                  pl.BlockSpec((1, tn), lambda j, i: (0, j + nj))],
        out_specs=pl.BlockSpec((tm, tn), lambda j, i: (i, j)),
        compiler_params=_cp("parallel", "arbitrary"),
    )(h, h, w_up, w_up, cw, cw, cb, cb)


PAIRS = 8
IDX_PAIRS = 4
TQ = 128
TK = 512


def _dsa_proj_kernel(h_ref, wq_ref, wk_ref, wvt_ref, wi_ref, gq_ref, gk_ref, gki_ref,
                     q_ref, k_ref, vt_ref, qi_ref, ki_ref, wi_out_ref):
    h = h_ref[...]
    tm = h.shape[0]

    ones = _seg64_ones()

    def head_norm(y, g_ref, o_ref):
        for p2 in range(PAIRS // 2):
            sl = slice(p2 * SEG_W, (p2 + 1) * SEG_W)
            yp = y[:, sl]
            ms = _seg64_sum(yp * yp, ones) * (1.0 / HEAD)
            yn = (yp * lax.rsqrt(ms + EPS) * g_ref[:, sl]).astype(o_ref.dtype)
            o_ref[0, 2 * p2] = yn[:, :LANES]
            o_ref[0, 2 * p2 + 1] = yn[:, LANES:]

    head_norm(_dot(h, wq_ref[...]), gq_ref, q_ref)
    head_norm(_dot(h, wk_ref[...]), gk_ref, k_ref)
    vt = _nt(wvt_ref[...], h)
    vt_ref[0, :, :LANES, :] = vt.reshape(PAIRS, LANES, tm).astype(vt_ref.dtype)
    vt_ref[0, :, LANES:, :] = jnp.ones((PAIRS, vt_ref.shape[2] - LANES, tm), vt_ref.dtype)
    idx = _dot(h, wi_ref[...])
    for p in range(IDX_PAIRS):
        qi_ref[0, p] = (idx[:, p * LANES:(p + 1) * LANES] * (HEAD ** -0.5)).astype(qi_ref.dtype)
    kw = idx[:, IDX_PAIRS * LANES:]
    ms = _seg64_sum(kw * kw, ones)[:, :LANES] * (1.0 / HEAD)
    ki_ref[0] = (kw[:, :LANES] * lax.rsqrt(ms + EPS) * gki_ref[...]).astype(ki_ref.dtype)
    wi_out_ref[0] = idx[:, (IDX_PAIRS + 1) * LANES:] * (2 * IDX_PAIRS) ** -0.5


def dsa_proj(h, bsz, seq, w_in, q_gain, k_gain, kidx_gain, tm=256):
    t, d = h.shape
    n_idx = 2 * IDX_PAIRS
    wq = w_in[:, :d].astype(BF16)
    wk = w_in[:, d:2 * d].astype(BF16)
    wvt = w_in[:, 2 * d:3 * d].T.astype(BF16)
    o = 3 * d
    w_qi = w_in[:, o:o + n_idx * HEAD]
    w_ki = w_in[:, o + n_idx * HEAD:o + n_idx * HEAD + HEAD]
    w_wi = w_in[:, o + n_idx * HEAD + HEAD:]
    w_wi = jnp.pad(w_wi, ((0, 0), (0, LANES - n_idx)))
    wi = jnp.concatenate([w_qi, w_ki, w_ki, w_wi], axis=1).astype(BF16)
    gq = (jnp.tile(q_gain, d // HEAD) * (HEAD ** -0.5 * math.log2(math.e))).reshape(1, d)
    gk = jnp.tile(k_gain, d // HEAD).reshape(1, d)
    gki = jnp.tile(kidx_gain, 2).reshape(1, LANES)
    nb = seq // tm
    fixed = lambda b, i: (0, 0)
    return pl.pallas_call(
        _dsa_proj_kernel,
        out_shape=(jax.ShapeDtypeStruct((bsz, PAIRS, seq, LANES), BF16),
                   jax.ShapeDtypeStruct((bsz, PAIRS, seq, LANES), BF16),
                   jax.ShapeDtypeStruct((bsz, PAIRS, VT_ROWS, seq), BF16),
                   jax.ShapeDtypeStruct((bsz, IDX_PAIRS, seq, LANES), BF16),
                   jax.ShapeDtypeStruct((bsz, seq, LANES), BF16),
                   jax.ShapeDtypeStruct((bsz, seq, LANES), F32)),
        grid=(bsz, nb),
        in_specs=[pl.BlockSpec((tm, d), lambda b, i: (b * nb + i, 0)),
                  pl.BlockSpec((d, d), fixed), pl.BlockSpec((d, d), fixed), pl.BlockSpec((d, d), fixed),
                  pl.BlockSpec((d, wi.shape[1]), fixed),
                  pl.BlockSpec((1, d), fixed), pl.BlockSpec((1, d), fixed), pl.BlockSpec((1, LANES), fixed)],
        out_specs=(pl.BlockSpec((1, PAIRS, tm, LANES), lambda b, i: (b, 0, i, 0)),
                   pl.BlockSpec((1, PAIRS, tm, LANES), lambda b, i: (b, 0, i, 0)),
                   pl.BlockSpec((1, PAIRS, VT_ROWS, tm), lambda b, i: (b, 0, 0, i)),
                   pl.BlockSpec((1, IDX_PAIRS, tm, LANES), lambda b, i: (b, 0, i, 0)),
                   pl.BlockSpec((1, tm, LANES), lambda b, i: (b, i, 0)),
                   pl.BlockSpec((1, tm, LANES), lambda b, i: (b, i, 0))),
        compiler_params=_cp("parallel", "parallel"),
    )(h, wq, wk, wvt, wi, gq, gk, gki)


def _pair_split(x):
    lane = lax.broadcasted_iota(I32, x.shape, 1)
    zero = jnp.zeros_like(x)
    return jnp.concatenate([jnp.where(lane < HEAD, x, zero), jnp.where(lane >= HEAD, x, zero)], axis=0)


HI16 = -65536
EXP_MASK = 0x7F800000
MIN_NORMAL = 0x00800000


def _dsa_select_kernel(qi_ref, ki_ref, wi_ref, mask_ref, keys_ref, hkeys_ref, *, seq, topk):
    j = pl.program_id(1)
    lane = lax.broadcasted_iota(I32, (1, TQ), 1)
    limq = j * TQ + (lane // CHUNK + 1) * CHUNK
    nkt = (j * TQ + TQ + TK - 1) // TK
    wit = wi_ref[0].T
    w_pairs = [_pair_split(qi_ref[0, p]) for p in range(IDX_PAIRS)]
    kiota = lax.broadcasted_iota(I32, (TK, TQ), 0)

    def score_body(kt, carry):
        off = pl.multiple_of(kt * TK, TK)
        kit = ki_ref[0, pl.ds(off, TK), :]
        s = jnp.zeros((TK, TQ), F32)
        for p in range(IDX_PAIRS):
            lg = _nt(kit, w_pairs[p])
            s = s + jnp.maximum(lg[:, :TQ], 0.0) * wit[2 * p:2 * p + 1, :]
            s = s + jnp.maximum(lg[:, TQ:], 0.0) * wit[2 * p + 1:2 * p + 2, :]
        s = jnp.where(off + kiota < limq, s, NEG_INF)
        bits = pltpu.bitcast(s, I32)
        bits = jnp.where((bits & EXP_MASK) == 0, 0, bits)
        keys_ref[pl.ds(off, TK), :] = jnp.where(bits < 0, bits ^ 0x7FFFFFFF, bits)
        hkeys_ref[pl.ds(off, TK), :] = pltpu.bitcast(bits & HI16, F32).astype(BF16)
        return carry

    lax.fori_loop(0, nkt, score_body, 0)

    def count_hi(cand_b):
        one = jnp.ones((TK, TQ), BF16)
        zero = jnp.zeros((TK, TQ), BF16)

        def body(i, acc):
            off = pl.multiple_of(i * TK, TK)
            hit = jnp.where(hkeys_ref[pl.ds(off, TK), :] >= cand_b, one, zero).reshape(TK // 16, 16, TQ)
            parts = [hit[g] for g in range(TK // 16)]
            while len(parts) > 1:
                parts = [a + b for a, b in zip(parts[0::2], parts[1::2])]
            return acc + parts[0].astype(F32)
        acc = lax.fori_loop(0, nkt, body, jnp.zeros((16, TQ), F32))
        return acc.sum(axis=0, keepdims=True)

    def count(pred):
        def body(i, acc):
            off = pl.multiple_of(i * TK, TK)
            hit = jnp.where(pred(keys_ref[pl.ds(off, TK), :], off), 1, 0)
            return acc + hit.reshape(TK // 8, 8, TQ).sum(axis=0)
        acc = lax.fori_loop(0, nkt, body, jnp.zeros((8, TQ), I32))
        return acc.sum(axis=0, keepdims=True)

    def search(_):
        def hi_body(t, tu):
            cand_u = tu | jnp.left_shift(jnp.int32(1), 31 - t)
            cand_s = cand_u ^ INT_MIN
            fbits = jnp.where(cand_s < 0, cand_s ^ 0x7FFFFFFF, cand_s) & HI16
            fbits = jnp.where((fbits & EXP_MASK) == 0, jnp.where(fbits > 0, MIN_NORMAL, 0), fbits)
            c = count_hi(pltpu.bitcast(fbits, F32).astype(BF16))
            return jnp.where(c >= topk, cand_u, tu)

        def lo_body(t, tu):
            cand_u = tu | jnp.left_shift(jnp.int32(1), 31 - t)
            cand_s = cand_u ^ INT_MIN
            c = count(lambda kb, off: kb >= cand_s)
            return jnp.where(c >= topk, cand_u, tu)
        tu = lax.fori_loop(0, 16, hi_body, jnp.zeros((1, TQ), I32))
        tu = lax.fori_loop(16, 32, lo_body, tu)
        return tu ^ INT_MIN

    ts = lax.cond(j * TQ + TQ > topk, search, lambda _: jnp.full((1, TQ), INT_MIN, I32), 0)

    n_ge = count(lambda kb, off: kb >= ts)

    @pl.when(jnp.max(n_ge) > topk)
    def _break_ties():
        r = topk - count(lambda kb, off: kb > ts)

        def bit_body(t, p):
            cand = p | jnp.left_shift(jnp.int32(1), (seq.bit_length() - 1) - t)
            c = count(lambda kb, off: (kb == ts) & (off + kiota < cand))
            return jnp.where(c < r, cand, p)
        p_last = lax.fori_loop(0, seq.bit_length(), bit_body, jnp.zeros((1, TQ), I32))

        def demote(kt, carry):
            off = pl.multiple_of(kt * TK, TK)
            kb = keys_ref[pl.ds(off, TK), :]
            keys_ref[pl.ds(off, TK), :] = jnp.where((kb == ts) & (off + kiota > p_last), ts - 1, kb)
            return carry
        lax.fori_loop(0, nkt, demote, 0)

    def out_body(kt, carry):
        off = pl.multiple_of(kt * TK, TK)

        @pl.when(kt < nkt)
        def _():
            kb = keys_ref[pl.ds(off, TK), :]
            sel = (kb >= ts) & (off + kiota < limq)
            mask_ref[0, 0, pl.ds(off, TK), :] = jnp.where(sel, 1, 0).astype(mask_ref.dtype)

        @pl.when(kt >= nkt)
        def _():
            mask_ref[0, 0, pl.ds(off, TK), :] = jnp.zeros((TK, TQ), mask_ref.dtype)
        return carry

    lax.fori_loop(0, seq // TK, out_body, 0)


def dsa_select(qi, ki, wi, topk):
    bsz, _, seq, _ = qi.shape
    nq = seq // TQ
    return pl.pallas_call(
        functools.partial(_dsa_select_kernel, seq=seq, topk=topk),
        out_shape=jax.ShapeDtypeStruct((bsz, nq, seq, TQ), jnp.int8),
        grid=(bsz, nq),
        in_specs=[pl.BlockSpec((1, IDX_PAIRS, TQ, LANES), lambda b, j: (b, 0, j, 0)),
                  pl.BlockSpec((1, seq, LANES), lambda b, j: (b, 0, 0)),
                  pl.BlockSpec((1, TQ, LANES), lambda b, j: (b, j, 0))],
        out_specs=pl.BlockSpec((1, 1, seq, TQ), lambda b, j: (b, j, 0, 0)),
        scratch_shapes=[pltpu.VMEM((seq, TQ), I32), pltpu.VMEM((seq, TQ), BF16)],
        compiler_params=_cp("parallel", "parallel"),
    )(qi, ki, wi)


VT_ROWS = LANES + 16
S_CHUNK = 64
S_AHEAD = 3
S_SLOTS = S_AHEAD + 1


def _dsa_attn_kernel(jmap_ref, ktmap_ref, q_ref, k_ref, vt_ref, mask_ref, o_ref,
                     qm_ref, m_ref, l_ref, acc_ref, bias_ref, s_ref, p_ref):
    step = pl.program_id(1)
    j = jmap_ref[step]
    kt = ktmap_ref[step]
    last = (j * TQ + TQ - 1) // TK

    @pl.when(kt == 0)
    def _init():
        for p in range(PAIRS):
            qm_ref[p] = _pair_split(q_ref[0, p].astype(F32)).T.astype(BF16)
        m_ref[...] = jnp.full(m_ref.shape, -jnp.inf, F32)
        l_ref[...] = jnp.zeros(l_ref.shape, F32)
        acc_ref[...] = jnp.zeros(acc_ref.shape, F32)

    bias = (mask_ref[0, 0].astype(F32) - 1.0) * 1e30
    bias_ref[:, :TQ] = bias
    bias_ref[:, TQ:] = bias

    def scores(p):
        m_tile = None
        for hf in range(2):
            rows = slice(hf * (TK // 2), (hf + 1) * (TK // 2))
            s = _dot(k_ref[0, p, rows, :], qm_ref[p]) + bias_ref[rows, :]
            s_ref[p % S_SLOTS, rows, :] = s
            m_half = jnp.max(s, axis=0, keepdims=True)
            m_tile = m_half if m_tile is None else jnp.maximum(m_tile, m_half)
        return m_tile

    m_tiles = [scores(p) for p in range(S_AHEAD)]
    for p in range(PAIRS):
        slot = p % S_SLOTS
        m_old = m_ref[p]
        m_new = jnp.maximum(m_old, m_tiles[p])
        alpha = jnp.exp2(m_old - m_new)
        m_ref[p] = m_new
        if p + S_AHEAD < PAIRS:
            m_tiles.append(scores(p + S_AHEAD))
        for c in range(TK // S_CHUNK):
            rows = slice(c * S_CHUNK, (c + 1) * S_CHUNK)
            p_ref[slot, rows, :] = jnp.exp2(s_ref[slot, rows, :] - m_new).astype(BF16)
        pv = _dot(vt_ref[0, p], p_ref[slot])
        acc_ref[p] = acc_ref[p] * alpha + pv[:LANES]
        l_ref[p] = l_ref[p] * alpha + pv[LANES:LANES + 1]

    @pl.when(kt == last)
    def _finish():
        for p in range(PAIRS):
            a = acc_ref[p] / l_ref[p]
            ot = jnp.concatenate([a[0:HEAD, 0:TQ], a[HEAD:2 * HEAD, TQ:2 * TQ]], axis=0)
            o_ref[0, :, p * LANES:(p + 1) * LANES] = ot.T.astype(o_ref.dtype)


def dsa_attn(q, k, vt, mask):
    bsz, _, seq, _ = q.shape
    nq = seq // TQ
    visits = [(j, t) for j in range(nq) for t in range((j * TQ + TQ - 1) // TK + 1)]
    jmap = jnp.asarray([jt[0] for jt in visits], I32)
    ktmap = jnp.asarray([jt[1] for jt in visits], I32)
    grid_spec = pltpu.PrefetchScalarGridSpec(
        num_scalar_prefetch=2,
        grid=(bsz, len(visits)),
        in_specs=[pl.BlockSpec((1, PAIRS, TQ, LANES), lambda b, s, jm, km: (b, 0, jm[s], 0)),
                  pl.BlockSpec((1, PAIRS, TK, LANES), lambda b, s, jm, km: (b, 0, km[s], 0)),
                  pl.BlockSpec((1, PAIRS, VT_ROWS, TK), lambda b, s, jm, km: (b, 0, 0, km[s])),
                  pl.BlockSpec((1, 1, TK, TQ), lambda b, s, jm, km: (b, jm[s], km[s], 0))],
        out_specs=pl.BlockSpec((1, TQ, PAIRS * LANES), lambda b, s, jm, km: (b, jm[s], 0)),
        scratch_shapes=[pltpu.VMEM((PAIRS, LANES, 2 * TQ), BF16),
                        pltpu.VMEM((PAIRS, 1, 2 * TQ), F32),
                        pltpu.VMEM((PAIRS, 1, 2 * TQ), F32),
                        pltpu.VMEM((PAIRS, LANES, 2 * TQ), F32),
                        pltpu.VMEM((TK, 2 * TQ), F32),
                        pltpu.VMEM((S_SLOTS, TK, 2 * TQ), F32),
                        pltpu.VMEM((S_SLOTS, TK, 2 * TQ), BF16)])
    return pl.pallas_call(
        _dsa_attn_kernel,
        out_shape=jax.ShapeDtypeStruct((bsz, seq, PAIRS * LANES), BF16),
        grid_spec=grid_spec,
        compiler_params=_cp("parallel", "arbitrary"),
    )(jmap, ktmap, q, k, vt, mask)


def dsa_mixer(h, bsz, seq, w_in, q_gain, k_gain, kidx_gain):
    assert seq % TK == 0 and TK % TQ == 0
    topk = min(TOPK_MAX, seq // 4)
    q, k, vt, qi, ki, wi = dsa_proj(h, bsz, seq, w_in, q_gain, k_gain, kidx_gain)
    mask = dsa_select(qi, ki, wi, topk)
    return dsa_attn(q, k, vt, mask).reshape(bsz * seq, PAIRS * LANES)


RW_PREV = 8
RW_C = 64
RW_CH = 8


def _rwkv_proj_kernel(x_ref, xp_ref, gn_ref, mu_ref, wr_ref, wk_ref, wv_ref, w1_ref, w2_ref, a1_ref, a2_ref,
                      g1_ref, g2_ref, w0_ref, a0_ref, kk_ref, ka_ref,
                      r_out, lw_out, k_out, v_out, kk_out, b_out, g_out, *, tiles_per_seq):
    i = pl.program_id(0)
    gn = gn_ref[...]
    h = _rms(x_ref[...], gn)
    hp = _rms(xp_ref[...], gn)
    hp_last = jnp.where((i % tiles_per_seq) == 0, 0.0, hp[RW_PREV - 1:RW_PREV, :])
    rows = lax.broadcasted_iota(I32, h.shape, 0)
    dh = jnp.where(rows == 0, hp_last, pltpu.roll(h, 1, 0)) - h
    mu = mu_ref[...]
    xs = lambda n: (h + dh * mu[n:n + 1, :]).astype(BF16)
    r = _dot(xs(0), wr_ref[...])
    k = _dot(xs(1), wk_ref[...])
    v_out[...] = _dot(xs(2), wv_ref[...])
    wl = w0_ref[...] + _dot(jnp.tanh(_dot(xs(3), w1_ref[...])).astype(BF16), w2_ref[...])
    z = -wl
    w_log = -(jnp.maximum(z, 0.0) + jnp.log(1.0 + jnp.exp(-jnp.abs(z)))) - 0.5
    lw_out[...] = -jnp.exp(w_log)
    a = jax.nn.sigmoid(a0_ref[...] + _dot(_dot(xs(4), a1_ref[...]).astype(BF16), a2_ref[...]))
    g_out[...] = _dot(jax.nn.sigmoid(_dot(xs(5), g1_ref[...])).astype(BF16), g2_ref[...])
    r_out[...] = r
    k_out[...] = k * (1.0 + (a - 1.0) * ka_ref[...])
    kk = k * kk_ref[...]
    ones = _seg64_ones()
    for p in range(kk.shape[1] // SEG_W):
        sl = slice(p * SEG_W, (p + 1) * SEG_W)
        kp = kk[:, sl]
        kn = kp / jnp.maximum(jnp.sqrt(_seg64_sum(kp * kp, ones)), 1e-12)
        kk_out[:, sl] = kn
        b_out[:, sl] = kn * a[:, sl]


def rwkv_proj(x2, seq, gn, mu, w_rkv, w0, w1, w2, a0, a1, a2, g1, g2, k_k, k_a, tm=256):
    t, d = x2.shape
    pad_c = lambda w: jnp.pad(w, ((0, 0), (0, LANES - w.shape[1]))).astype(BF16)
    pad_r = lambda w: jnp.pad(w, ((0, LANES - w.shape[0]), (0, 0))).astype(BF16)
    row = lambda i: (i, 0)
    fixed = lambda i: (0, 0)
    vec = lambda a: a.reshape(1, d)
    full = lambda a: pl.BlockSpec(a.shape, fixed)
    args = [x2, x2, vec(gn), jnp.pad(mu, ((0, 2), (0, 0))),
            w_rkv[0].astype(BF16), w_rkv[1].astype(BF16), w_rkv[2].astype(BF16),
            pad_c(w1), pad_r(w2), pad_c(a1), pad_r(a2), g1.astype(BF16), g2.astype(BF16),
            vec(w0), vec(a0), vec(k_k), vec(k_a)]
    in_specs = [pl.BlockSpec((tm, d), row),
                pl.BlockSpec((RW_PREV, d), lambda i: (jnp.maximum(i * (tm // RW_PREV) - 1, 0), 0))]
    in_specs += [full(a) for a in args[2:]]
    return pl.pallas_call(
        functools.partial(_rwkv_proj_kernel, tiles_per_seq=seq // tm),
        out_shape=tuple(jax.ShapeDtypeStruct((t, d), F32) for _ in range(7)),
        grid=(t // tm,),
        in_specs=in_specs,
        out_specs=tuple(pl.BlockSpec((tm, d), row) for _ in range(7)),
        compiler_params=_cp("parallel"),
    )(*args)


def _bdot(a, b):
    return _dot(a.astype(BF16), b.astype(BF16))


def _rwkv_chunk_terms(r, lw, k2, v, kk, b, tick):
    c = RW_C
    cat = jnp.concatenate
    each = lambda f, *ls: [f(*a) for a in zip(*ls)]
    row = lax.broadcasted_iota(I32, (2 * c, 2 * c), 0)
    col = lax.broadcasted_iota(I32, (2 * c, 2 * c), 1)
    r_c = lax.broadcasted_iota(I32, (c, c), 0)
    c_c = lax.broadcasted_iota(I32, (c, c), 1)
    tri = jnp.where(c_c <= r_c, 1.0, 0.0).astype(BF16)

    def cum(lw_):
        hi = lw_.astype(BF16)
        rem = lw_ - hi.astype(F32)
        mid = rem.astype(BF16)
        lo = (rem - mid.astype(F32)).astype(BF16)
        return _dot(tri, hi) + _dot(tri, mid) + _dot(tri, lo)
    cs = each(cum, lw)
    e_pos = each(jnp.exp, cs)
    e_neg = each(lambda x: jnp.exp(-x), cs)
    mul = lambda x, y: x * y
    rt = each(mul, r, e_pos)
    kt = each(mul, k2, e_neg)
    bt = each(mul, b, e_neg)
    kkt = each(lambda x, s, l: x * jnp.exp(s - l), kk, cs, lw)
    g_end = each(lambda e: e[c - 1:c, :], e_pos)
    khat = each(mul, kt, g_end)
    bhat = each(mul, bt, g_end)
    lane = lax.broadcasted_iota(I32, (c, LANES), 1)
    t_row = lax.broadcasted_iota(I32, (c, LANES), 0)
    m0 = lane < HEAD
    h0 = lambda x: jnp.where(m0, x, 0.0)
    h1 = lambda x: jnp.where(m0, 0.0, x)
    g0 = each(lambda kq, rr, bb, kk_: _nt(cat([h0(kq), h0(rr)]).astype(BF16), cat([bb, kk_]).astype(BF16)),
              kkt, rt, bt, kt)
    tick()
    g1 = each(lambda kq, rr, bb, kk_: _nt(cat([h1(rr), h1(kq)]).astype(BF16), cat([kk_, bb]).astype(BF16)),
              kkt, rt, bt, kt)
    top, left = row < c, col < c
    m_ab0, m_ab1 = top & left & (col < row), ~top & ~left & (col < row)
    m_ak0, m_ak1 = top & ~left & (col - c < row), ~top & left & (col < row - c)
    a_b = each(lambda x0, x1: jnp.where(m_ab0, x0, 0.0) + jnp.where(m_ab1, x1, 0.0), g0, g1)
    a_k = each(lambda x0, x1: jnp.where(m_ak0, x0, 0.0) + jnp.where(m_ak1, x1, 0.0), g0, g1)
    eye = jnp.where(row == col, 1.0, 0.0)
    x = each(lambda a: eye - a, a_b)
    pw = each(lambda a: _bdot(a, a), a_b)
    tick()
    for it in range(5):
        x = each(lambda xx, pp: xx + _bdot(xx, pp), x, pw)
        tick()
        if it < 4:
            pw = each(lambda pp: _bdot(pp, pp), pw)
    v0, v1 = each(h0, v), each(h1, v)
    akv = each(lambda a, va, vb: _bdot(a, cat([vb, va])), a_k, v0, v1)
    tick()
    wu = each(lambda xx, kq, av: _bdot(xx, cat([cat([h0(kq), h1(kq)]), av], axis=1)), x, kkt, akv)
    incl = jnp.where(m0, lane, lane - HEAD) <= t_row
    zeros = jnp.zeros((c, LANES), F32)
    rhs0 = each(lambda w_, va: cat([-w_[0:c], cat([zeros, va], axis=1)]), wu, v0)
    rhs1 = each(lambda w_, vb: cat([cat([zeros, vb], axis=1), -w_[c:2 * c]]), wu, v1)
    o0 = each(lambda g, rh: _bdot(jnp.where(incl, g[c:2 * c], 0.0), rh), g0, rhs0)
    o1 = each(lambda g, rh: _bdot(jnp.where(incl, g[0:c], 0.0), rh), g1, rhs1)
    rq = each(lambda rr, a0, a1: rr + a0[:, :LANES] + a1[:, :LANES], rt, o0, o1)
    yin = each(lambda a0, a1: a0[:, LANES:] + a1[:, LANES:], o0, o1)
    pd = each(lambda bh, kh, ra, rb: _tn(cat([h0(bh), h0(kh), h1(kh), h1(bh)]).astype(BF16),
                                         cat([ra, rb]).astype(BF16)), bhat, khat, rhs0, rhs1)
    phi = each(lambda p_, ge: p_[:, :LANES] + jnp.where(row == col, jnp.broadcast_to(ge, (2 * c, LANES)), 0.0),
               pd, g_end)
    return [(a, b_, c_, p_[:, LANES:]) for a, b_, c_, p_ in zip(rq, yin, phi, pd)]


def _rwkv_scan_kernel(r_ref, lw_ref, k_ref, v_ref, kk_ref, b_ref, y_ref, h_ref, lhs_ref, yin_ref, dh_ref):
    s = pl.program_id(2)
    cur = s % 2
    prev = 1 - cur
    rows = [slice(ci * RW_C, (ci + 1) * RW_C) for ci in range(RW_CH)]

    @pl.when(s == 0)
    def _():
        lhs_ref[prev] = jnp.zeros(lhs_ref.shape[1:], lhs_ref.dtype)
        yin_ref[prev] = jnp.zeros(yin_ref.shape[1:], F32)
        dh_ref[prev] = jnp.zeros(dh_ref.shape[1:], F32)
        h_ref[...] = jnp.zeros(h_ref.shape, F32)

    state = [jnp.where(s <= 1, 0.0, h_ref[...])]
    links = iter(range(RW_CH))

    def recurrence_step():
        ci = next(links, None)
        if ci is None:
            return
        both = _dot(lhs_ref[prev, ci], state[0].astype(BF16))
        y_ref[0, rows[ci], :] = both[0:RW_C] + yin_ref[prev, ci]
        state[0] = both[RW_C:] + dh_ref[prev, ci]

    ld = lambda ref: [ref[0, sl, :] for sl in rows]
    terms = _rwkv_chunk_terms(ld(r_ref), ld(lw_ref), ld(k_ref), ld(v_ref), ld(kk_ref), ld(b_ref), recurrence_step)
    for _ in range(RW_CH):
        recurrence_step()
    h_ref[...] = state[0]
    for ci, (rq, yin, phi, dh) in enumerate(terms):
        lhs_ref[cur, ci] = jnp.concatenate([rq, phi]).astype(BF16)
        yin_ref[cur, ci] = yin
        dh_ref[cur, ci] = dh


def rwkv_scan(r, lw, k2, v, kk, b, bsz, seq):
    d = r.shape[-1]
    rows = RW_C * RW_CH
    nblk = seq // rows
    in_spec = pl.BlockSpec((1, rows, LANES), lambda bb, p, s: (bb, jnp.minimum(s, nblk - 1), p))
    out_spec = pl.BlockSpec((1, rows, LANES), lambda bb, p, s: (bb, jnp.maximum(s - 1, 0), p))
    shp = lambda a: a.reshape(bsz, seq, d)
    return pl.pallas_call(
        _rwkv_scan_kernel,
        out_shape=jax.ShapeDtypeStruct((bsz, seq, d), F32),
        grid=(bsz, d // LANES, nblk + 1),
        in_specs=[in_spec] * 6,
        out_specs=out_spec,
        scratch_shapes=[pltpu.VMEM((LANES, LANES), F32),
                        pltpu.VMEM((2, RW_CH, RW_C + LANES, LANES), BF16),
                        pltpu.VMEM((2, RW_CH, RW_C, LANES), F32),
                        pltpu.VMEM((2, RW_CH, LANES, LANES), F32)],
        compiler_params=_cp("parallel", "parallel", "arbitrary"),
    )(shp(r), shp(lw), shp(k2), shp(v), shp(kk), shp(b)).reshape(bsz * seq, d)


def _rwkv_post_kernel(y_ref, r_ref, k_ref, v_ref, g_ref, lng_ref, lnb_ref, rk_ref, w_ref, x_ref, gn_ref,
                      xo_ref, ho_ref, o_scr):
    ones = _seg64_ones()
    for p in range(y_ref.shape[1] // SEG_W):
        sl = slice(p * SEG_W, (p + 1) * SEG_W)
        y = y_ref[:, sl]
        dv = y - _seg64_sum(y, ones) * (1.0 / HEAD)
        var = _seg64_sum(dv * dv, ones) * (1.0 / HEAD)
        yn = dv * lax.rsqrt(var + GN_EPS) * lng_ref[:, sl] + lnb_ref[:, sl]
        bonus = _seg64_sum(r_ref[:, sl] * k_ref[:, sl] * rk_ref[:, sl], ones) * v_ref[:, sl]
        o_scr[:, sl] = ((yn + bonus) * g_ref[:, sl]).astype(o_scr.dtype)
    xn = x_ref[...] + _dot(o_scr[...], w_ref[...])
    xo_ref[...] = xn
    ho_ref[...] = _rms(xn, gn_ref[...]).astype(ho_ref.dtype)


def rwkv_post(y, r, k2, v, g, ln_g, ln_b, r_k, w_out, x2, gn, tm=256):
    t, d = x2.shape
    row = lambda i: (i, 0)
    fixed = lambda i: (0, 0)
    tile = pl.BlockSpec((tm, d), row)
    vecs = pl.BlockSpec((1, d), fixed)
    return pl.pallas_call(
        _rwkv_post_kernel,
        out_shape=(jax.ShapeDtypeStruct((t, d), F32), jax.ShapeDtypeStruct((t, d), BF16)),
        grid=(t // tm,),
        in_specs=[tile] * 5 + [vecs] * 3 + [pl.BlockSpec((d, d), fixed), tile, vecs],
        out_specs=(tile, tile),
        scratch_shapes=[pltpu.VMEM((tm, d), BF16)],
        compiler_params=_cp("parallel"),
    )(y, r, k2, v, g, ln_g.reshape(1, d), ln_b.reshape(1, d), r_k.reshape(1, d), w_out.astype(BF16), x2,
      gn.reshape(1, d))


S5_L = 16


def _split(a):
    hi = a.astype(BF16)
    return hi, (a - hi.astype(F32)).astype(BF16)


def _dot3(a, b_hi, b_lo):
    a_hi, a_lo = _split(a)
    return _dot(a_hi, b_hi) + _dot(a_lo, b_hi) + _dot(a_hi, b_lo)


def _s5_local_kernel(u_ref, grh_ref, grl_ref, gih_ref, gil_ref, xr_ref, xi_ref):
    u0, u1 = u_ref[0], u_ref[1]
    xr_ref[...] = _dot3(u0, grh_ref[0, 0], grl_ref[0, 0]) + _dot3(u1, grh_ref[0, 1], grl_ref[0, 1])
    xi_ref[...] = _dot3(u0, gih_ref[0, 0], gil_ref[0, 0]) + _dot3(u1, gih_ref[0, 1], gil_ref[0, 1])


def _s5_carry_kernel(er_ref, ei_ref, lr_ref, li_ref, pr_ref, pi_ref):
    lr = lr_ref[...]
    li = li_ref[...]

    def body(n, st):
        sr, si = st
        pr_ref[n] = sr
        pi_ref[n] = si
        return (lr * sr - li * si + er_ref[n], lr * si + li * sr + ei_ref[n])
    zero = jnp.zeros(lr.shape, F32)
    lax.fori_loop(0, er_ref.shape[0], body, (zero, zero))


def _s5_out_kernel(u_ref, pr_ref, pi_ref, kh_ref, kl_ref, erh_ref, erl_ref, eih_ref, eil_ref, y_ref):
    pr = pr_ref[...]
    pi = pi_ref[...]
    for i in range(2):
        y_ref[i] = (_dot3(u_ref[i], kh_ref[i], kl_ref[i]) + _dot3(pr, erh_ref[0, i], erl_ref[0, i])
                    + _dot3(pi, eih_ref[0, i], eil_ref[0, i]))


def s5_ssm(h3, a_re, a_im, log_step, b_re, b_im, c_re, c_im):
    bsz, seq, d = h3.shape
    ng, ns = a_re.shape
    gc = d // ng
    L = S5_L
    nc = seq // L
    n = bsz * nc
    step = jnp.exp(log_step.astype(F32))[:, None]
    lam = lax.complex(a_re.astype(F32), a_im.astype(F32))
    lam_bar = jnp.exp(lam * step)
    b_bar = ((lam_bar - 1.0) / lam)[..., None] * lax.complex(b_re.astype(F32), b_im.astype(F32))
    cc = lax.complex(c_re.astype(F32), c_im.astype(F32))
    pw = jnp.exp((lam * step)[:, None, :] * jnp.arange(L + 1, dtype=F32)[None, :, None])
    lag = jnp.arange(L)[None, :] - jnp.arange(L)[:, None]
    kfull = jnp.einsum('gcp,gstp,gpe->gsetc', cc, pw[:, jnp.clip(lag, 0, L)], b_bar)
    kmat = jnp.where((lag >= 0)[None, :, None, :, None], jnp.real(kfull), 0.0).reshape(ng, L * gc, L * gc)
    gfull = jnp.einsum('gsp,gpe->gsep', pw[:, L - 1 - jnp.arange(L)], b_bar).reshape(ng, L * gc, ns)
    efull = jnp.einsum('gcp,gtp->gptc', cc, pw[:, 1:]).reshape(ng, ns, L * gc)
    lam_l = pw[:, L]
    assert 2 * ns == LANES and ng % 16 == 0
    nq = ng // 2

    def cols(m):
        m4 = m.reshape(nq, 2, m.shape[1], ns)
        z = jnp.zeros_like(m4[:, 0])
        return jnp.stack([jnp.concatenate([m4[:, 0], z], -1), jnp.concatenate([z, m4[:, 1]], -1)], 1)

    def rows(m):
        m4 = m.reshape(nq, 2, ns, m.shape[2])
        z = jnp.zeros_like(m4[:, 0])
        return jnp.stack([jnp.concatenate([m4[:, 0], z], -2), jnp.concatenate([z, m4[:, 1]], -2)], 1)

    kh, kl = _split(kmat)
    grh, grl = _split(cols(jnp.real(gfull)))
    gih, gil = _split(cols(jnp.imag(gfull)))
    erh, erl = _split(rows(jnp.real(efull)))
    eih, eil = _split(rows(-jnp.imag(efull)))
    lr = jnp.real(lam_l).reshape(nq, LANES)
    li = jnp.imag(lam_l).reshape(nq, LANES)

    lw = L * gc
    u = h3.reshape(bsz, nc, L, ng, gc).transpose(3, 0, 1, 2, 4).reshape(ng, n, lw)
    pair3 = lambda q: (q, 0, 0)
    pair4 = lambda q: (q, 0, 0, 0)
    col = lambda q: (0, q)
    gspec = pl.BlockSpec((1, 2, lw, LANES), pair4)
    xr, xi = pl.pallas_call(
        _s5_local_kernel,
        out_shape=(jax.ShapeDtypeStruct((n, nq * LANES), F32),) * 2,
        grid=(nq,),
        in_specs=[pl.BlockSpec((2, n, lw), pair3), gspec, gspec, gspec, gspec],
        out_specs=(pl.BlockSpec((n, LANES), col),) * 2,
        compiler_params=_cp("parallel"),
    )(u, grh, grl, gih, gil)
    st_spec = pl.BlockSpec((nc, 8, LANES), lambda b, j: (b, j, 0))
    lam_spec = pl.BlockSpec((8, LANES), lambda b, j: (j, 0))
    pr, pi = pl.pallas_call(
        _s5_carry_kernel,
        out_shape=(jax.ShapeDtypeStruct((n, nq, LANES), F32),) * 2,
        grid=(bsz, nq // 8),
        in_specs=[st_spec, st_spec, lam_spec, lam_spec],
        out_specs=(st_spec, st_spec),
        compiler_params=_cp("parallel", "parallel"),
    )(xr.reshape(n, nq, LANES), xi.reshape(n, nq, LANES), lr, li)
    espec = pl.BlockSpec((1, 2, LANES, lw), pair4)
    kspec = pl.BlockSpec((2, lw, lw), pair3)
    y = pl.pallas_call(
        _s5_out_kernel,
        out_shape=jax.ShapeDtypeStruct((ng, n, lw), F32),
        grid=(nq,),
        in_specs=[pl.BlockSpec((2, n, lw), pair3), pl.BlockSpec((n, LANES), col), pl.BlockSpec((n, LANES), col),
                  kspec, kspec, espec, espec, espec, espec],
        out_specs=pl.BlockSpec((2, n, lw), pair3),
        compiler_params=_cp("parallel"),
    )(u, pr.reshape(n, nq * LANES), pi.reshape(n, nq * LANES), kh, kl, erh, erl, eih, eil)
    return y.reshape(ng, bsz, nc, L, gc).transpose(1, 2, 3, 0, 4).reshape(bsz * seq, d)


def _s5_glu_kernel(ys_ref, h_ref, d_ref, w_ref, b_ref, x_ref, gn_ref, xo_ref, ho_ref):
    y = ys_ref[...] + d_ref[...] * h_ref[...]
    gelu = 0.5 * y * (1.0 + jnp.tanh(math.sqrt(2.0 / math.pi) * (y + 0.044715 * (y * y * y))))
    z = _dot(gelu.astype(BF16), w_ref[...]) + b_ref[...]
    dm = x_ref.shape[1]
    xn = x_ref[...] + z[:, :dm] * jax.nn.sigmoid(z[:, dm:])
    xo_ref[...] = xn
    ho_ref[...] = _rms(xn, gn_ref[...]).astype(ho_ref.dtype)


def s5_glu(ys, h, d_skip, w_glu, b_glu, x2, gn, tm=256):
    t, d = x2.shape
    row = lambda i: (i, 0)
    fixed = lambda i: (0, 0)
    tile = pl.BlockSpec((tm, d), row)
    return pl.pallas_call(
        _s5_glu_kernel,
        out_shape=(jax.ShapeDtypeStruct((t, d), F32), jax.ShapeDtypeStruct((t, d), BF16)),
        grid=(t // tm,),
        in_specs=[tile, tile, pl.BlockSpec((1, d), fixed), pl.BlockSpec((d, 2 * d), fixed),
                  pl.BlockSpec((1, 2 * d), fixed), tile, pl.BlockSpec((1, d), fixed)],
        out_specs=(tile, tile),
        compiler_params=_cp("parallel"),
    )(ys, h, d_skip.reshape(1, d), w_glu.astype(BF16), b_glu.reshape(1, 2 * d), x2, gn.reshape(1, d))


def kernel(x, norm_mix, norm_ffn, dsa_w_in, dsa_q_norm, dsa_k_norm, dsa_kidx_norm, dsa_w_out, rwkv_mu, rwkv_w_rkv, rwkv_w0, rwkv_w1, rwkv_w2, rwkv_a0, rwkv_a1, rwkv_a2, rwkv_g1, rwkv_g2, rwkv_k_k, rwkv_k_a, rwkv_r_k, rwkv_ln_g, rwkv_ln_b, rwkv_w_out, s5_a_re, s5_a_im, s5_log_step, s5_b_re, s5_b_im, s5_c_re, s5_c_im, s5_d, s5_w_glu, s5_b_glu, ffn_w_up, ffn_conv_w, ffn_conv_b, ffn_w_down):
    bsz, seq, d = x.shape
    depth = norm_mix.shape[0]
    x2 = x.reshape(bsz * seq, d)
    for i in range(depth):
        kind, j = i % 3, i // 3
        if kind == 0:
            h = rmsnorm(x2, norm_mix[i], BF16)
            o = dsa_mixer(h, bsz, seq, dsa_w_in[j], dsa_q_norm[j], dsa_k_norm[j], dsa_kidx_norm[j])
            x2, h = mm_res_norm(o, dsa_w_out[j].astype(BF16), x2, norm_ffn[i], BF16)
        elif kind == 1:
            r, lw, k2, v, kk, b, g = rwkv_proj(x2, seq, norm_mix[i], rwkv_mu[j], rwkv_w_rkv[j], rwkv_w0[j],
                                               rwkv_w1[j], rwkv_w2[j], rwkv_a0[j], rwkv_a1[j], rwkv_a2[j],
                                               rwkv_g1[j], rwkv_g2[j], rwkv_k_k[j], rwkv_k_a[j])
            y = rwkv_scan(r, lw, k2, v, kk, b, bsz, seq)
            x2, h = rwkv_post(y, r, k2, v, g, rwkv_ln_g[j], rwkv_ln_b[j], rwkv_r_k[j].reshape(d), rwkv_w_out[j],
                              x2, norm_ffn[i])
        else:
            hf = rmsnorm(x2, norm_mix[i], F32)
            ys = s5_ssm(hf.reshape(bsz, seq, d), s5_a_re[j], s5_a_im[j], s5_log_step[j], s5_b_re[j], s5_b_im[j],
                        s5_c_re[j], s5_c_im[j])
            x2, h = s5_glu(ys, hf, s5_d[j], s5_w_glu[j], s5_b_glu[j], x2, norm_ffn[i])
        act = ffn_up(h, ffn_w_up[i].astype(BF16), ffn_conv_w[i], ffn_conv_b[i], seq)
        x2, _ = mm_res_norm(act, ffn_w_down[i].astype(BF16), x2, None, None)
    return x2.reshape(bsz, seq, d)
```

```python
import functools
import math

import jax
import jax.numpy as jnp
from jax import lax
from jax.experimental import pallas as pl
from jax.experimental.pallas import tpu as pltpu

F32 = jnp.float32
BF16 = jnp.bfloat16
I32 = jnp.int32

EPS = 1e-6
NEG_INF = -1e30
LANES = 128
HEAD = 64
CHUNK = 64
TOPK_MAX = 256
GN_EPS = 64e-5
INT_MIN = -(2 ** 31)

VMEM_LIMIT = 56 * 1024 * 1024


def _cp(*sem):
    return pltpu.CompilerParams(dimension_semantics=sem, vmem_limit_bytes=VMEM_LIMIT)


def _nt(a, b):
    return lax.dot_general(a, b, (((1,), (1,)), ((), ())), preferred_element_type=F32)


def _tn(a, b):
    return lax.dot_general(a, b, (((0,), (0,)), ((), ())), preferred_element_type=F32)


def _dot(a, b):
    return jnp.dot(a, b, preferred_element_type=F32)


def _rms(x, g):
    ms = jnp.mean(x * x, axis=-1, keepdims=True)
    return x * lax.rsqrt(ms + EPS) * g


SEG_W = 2 * LANES


def _seg64_ones():
    r = lax.broadcasted_iota(I32, (SEG_W, SEG_W), 0) // HEAD
    c = lax.broadcasted_iota(I32, (SEG_W, SEG_W), 1) // HEAD
    return jnp.where(r == c, 1.0, 0.0).astype(BF16)


def _seg64_sum(x, ones):
    hi = x.astype(BF16)
    lo = (x - hi.astype(F32)).astype(BF16)
    return _dot(hi, ones) + _dot(lo, ones)


def _norm_kernel(x_ref, g_ref, o_ref):
    o_ref[...] = _rms(x_ref[...], g_ref[...]).astype(o_ref.dtype)


def rmsnorm(x2, g, out_dtype, tm=1024):
    t, d = x2.shape
    return pl.pallas_call(
        _norm_kernel,
        out_shape=jax.ShapeDtypeStruct((t, d), out_dtype),
        grid=(t // tm,),
        in_specs=[pl.BlockSpec((tm, d), lambda i: (i, 0)),
                  pl.BlockSpec((1, d), lambda i: (0, 0))],
        out_specs=pl.BlockSpec((tm, d), lambda i: (i, 0)),
        compiler_params=_cp("parallel"),
    )(x2, g.reshape(1, d))


def _mm_res_norm_kernel(a_ref, w_ref, x_ref, g_ref, xo_ref, ho_ref):
    xn = x_ref[...] + _dot(a_ref[...], w_ref[...])
    xo_ref[...] = xn
    ho_ref[...] = _rms(xn, g_ref[...]).astype(ho_ref.dtype)


def _mm_res_kernel(a_ref, w_ref, x_ref, xo_ref):
    xo_ref[...] = x_ref[...] + _dot(a_ref[...], w_ref[...])


def mm_res_norm(a, w, x2, g, h_dtype, tm=512):
    t, k = a.shape
    d = w.shape[1]
    row = lambda i: (i, 0)
    fixed = lambda i: (0, 0)
    in_specs = [pl.BlockSpec((tm, k), row), pl.BlockSpec((k, d), fixed), pl.BlockSpec((tm, d), row)]
    if g is None:
        return pl.pallas_call(
            _mm_res_kernel,
            out_shape=jax.ShapeDtypeStruct((t, d), F32),
            grid=(t // tm,), in_specs=in_specs, out_specs=pl.BlockSpec((tm, d), row),
            compiler_params=_cp("parallel"),
        )(a, w, x2), None
    return pl.pallas_call(
        _mm_res_norm_kernel,
        out_shape=(jax.ShapeDtypeStruct((t, d), F32), jax.ShapeDtypeStruct((t, d), h_dtype)),
        grid=(t // tm,),
        in_specs=in_specs + [pl.BlockSpec((1, d), fixed)],
        out_specs=(pl.BlockSpec((tm, d), row), pl.BlockSpec((tm, d), row)),
        compiler_params=_cp("parallel"),
    )(a, w, x2, g.reshape(1, d))


PREV_ROWS = 16


FFN_TN = 256


def _ffn_up_kernel(h_ref, hp_ref, w_ref, cw_ref, cb_ref, o_ref, *, tiles_per_seq):
    i = pl.program_id(0)
    h = h_ref[...]
    hp = hp_ref[...]
    f = o_ref.shape[1]
    seq_start = (i % tiles_per_seq) == 0
    head = PREV_ROWS
    rows = lax.broadcasted_iota(I32, (head, FFN_TN), 0)

    def project(c):
        out = []
        for base in (0, f):
            cols = slice(base + c * FFN_TN, base + (c + 1) * FFN_TN)
            w = w_ref[:, cols]
            out.append((_dot(h, w), jnp.where(seq_start, 0.0, _dot(hp, w)), cols))
        return out

    def conv(u, up, cols):
        cw = cw_ref[:, cols]
        cb = cb_ref[:, cols]
        mix = lambda u2, u1, u0: cw[0:1, :] * u2 + cw[1:2, :] * u1 + cw[2:3, :] * u0 + cb
        full = mix(pltpu.roll(u, 2, 0), pltpu.roll(u, 1, 0), u)
        uh = u[0:head]
        last = up[PREV_ROWS - 1:PREV_ROWS, :]
        u1 = jnp.where(rows == 0, last, pltpu.roll(uh, 1, 0))
        u2 = jnp.where(rows == 0, up[PREV_ROWS - 2:PREV_ROWS - 1, :],
                       jnp.where(rows == 1, last, pltpu.roll(uh, 2, 0)))
        return full, mix(u2, u1, uh)

    n_chunks = f // FFN_TN
    nxt = project(0)
    for c in range(n_chunks):
        (ug, upg, cg), (uv, upv, cv) = nxt
        if c + 1 < n_chunks:
            nxt = project(c + 1)
        gate, gate_h = conv(ug, upg, cg)
        val, val_h = conv(uv, upv, cv)
        o_ref[:, cg] = (gate * jax.nn.sigmoid(gate) * val).astype(o_ref.dtype)
        o_ref[0:head, cg] = (gate_h * jax.nn.sigmoid(gate_h) * val_h).astype(o_ref.dtype)


def ffn_up(h, w_up, conv_w, conv_b, seq, tm=512):
    t, d = h.shape
    f = w_up.shape[1] // 2
    assert f % FFN_TN == 0 and seq % tm == 0
    cw = jnp.zeros((8, 2 * f), F32).at[:conv_w.shape[0]].set(conv_w)
    cb = conv_b.reshape(1, 2 * f)
    fixed = lambda i: (0, 0)
    return pl.pallas_call(
        functools.partial(_ffn_up_kernel, tiles_per_seq=seq // tm),
        out_shape=jax.ShapeDtypeStruct((t, f), BF16),
        grid=(t // tm,),
        in_specs=[pl.BlockSpec((tm, d), lambda i: (i, 0)),
                  pl.BlockSpec((PREV_ROWS, d), lambda i: (jnp.maximum(i * (tm // PREV_ROWS) - 1, 0), 0)),
                  pl.BlockSpec((d, 2 * f), fixed),
                  pl.BlockSpec((8, 2 * f), fixed),
                  pl.BlockSpec((1, 2 * f), fixed)],
        out_specs=pl.BlockSpec((tm, f), lambda i: (i, 0)),
        compiler_params=_cp("parallel"),
    )(h, h, w_up, cw, cb)


PAIRS = 8
IDX_PAIRS = 4
TQ = 128
TK = 512


def _dsa_proj_kernel(h_ref, wq_ref, wk_ref, wvt_ref, wi_ref, gq_ref, gk_ref, gki_ref,
                     q_ref, k_ref, vt_ref, qi_ref, ki_ref, wi_out_ref):
    h = h_ref[...]
    tm = h.shape[0]

    ones = _seg64_ones()

    def head_norm(y, g_ref, o_ref):
        for p2 in range(PAIRS // 2):
            sl = slice(p2 * SEG_W, (p2 + 1) * SEG_W)
            yp = y[:, sl]
            ms = _seg64_sum(yp * yp, ones) * (1.0 / HEAD)
            yn = (yp * lax.rsqrt(ms + EPS) * g_ref[:, sl]).astype(o_ref.dtype)
            o_ref[0, 2 * p2] = yn[:, :LANES]
            o_ref[0, 2 * p2 + 1] = yn[:, LANES:]

    head_norm(_dot(h, wq_ref[...]), gq_ref, q_ref)
    head_norm(_dot(h, wk_ref[...]), gk_ref, k_ref)
    vt = _nt(wvt_ref[...], h)
    vt_ref[0, :, :LANES, :] = vt.reshape(PAIRS, LANES, tm).astype(vt_ref.dtype)
    vt_ref[0, :, LANES:, :] = jnp.ones((PAIRS, vt_ref.shape[2] - LANES, tm), vt_ref.dtype)
    idx = _dot(h, wi_ref[...])
    for p in range(IDX_PAIRS):
        qi_ref[0, p] = (idx[:, p * LANES:(p + 1) * LANES] * (HEAD ** -0.5)).astype(qi_ref.dtype)
    kw = idx[:, IDX_PAIRS * LANES:]
    ms = _seg64_sum(kw * kw, ones)[:, :LANES] * (1.0 / HEAD)
    ki_ref[0] = (kw[:, :LANES] * lax.rsqrt(ms + EPS) * gki_ref[...]).astype(ki_ref.dtype)
    wi_out_ref[0] = idx[:, (IDX_PAIRS + 1) * LANES:] * (2 * IDX_PAIRS) ** -0.5


def dsa_proj(h, bsz, seq, w_in, q_gain, k_gain, kidx_gain, tm=256):
    t, d = h.shape
    n_idx = 2 * IDX_PAIRS
    wq = w_in[:, :d].astype(BF16)
    wk = w_in[:, d:2 * d].astype(BF16)
    wvt = w_in[:, 2 * d:3 * d].T.astype(BF16)
    o = 3 * d
    w_qi = w_in[:, o:o + n_idx * HEAD]
    w_ki = w_in[:, o + n_idx * HEAD:o + n_idx * HEAD + HEAD]
    w_wi = w_in[:, o + n_idx * HEAD + HEAD:]
    w_wi = jnp.pad(w_wi, ((0, 0), (0, LANES - n_idx)))
    wi = jnp.concatenate([w_qi, w_ki, w_ki, w_wi], axis=1).astype(BF16)
    gq = (jnp.tile(q_gain, d // HEAD) * (HEAD ** -0.5 * math.log2(math.e))).reshape(1, d)
    gk = jnp.tile(k_gain, d // HEAD).reshape(1, d)
    gki = jnp.tile(kidx_gain, 2).reshape(1, LANES)
    nb = seq // tm
    fixed = lambda b, i: (0, 0)
    return pl.pallas_call(
        _dsa_proj_kernel,
        out_shape=(jax.ShapeDtypeStruct((bsz, PAIRS, seq, LANES), BF16),
                   jax.ShapeDtypeStruct((bsz, PAIRS, seq, LANES), BF16),
                   jax.ShapeDtypeStruct((bsz, PAIRS, VT_ROWS, seq), BF16),
                   jax.ShapeDtypeStruct((bsz, IDX_PAIRS, seq, LANES), BF16),
                   jax.ShapeDtypeStruct((bsz, seq, LANES), BF16),
                   jax.ShapeDtypeStruct((bsz, seq, LANES), F32)),
        grid=(bsz, nb),
        in_specs=[pl.BlockSpec((tm, d), lambda b, i: (b * nb + i, 0)),
                  pl.BlockSpec((d, d), fixed), pl.BlockSpec((d, d), fixed), pl.BlockSpec((d, d), fixed),
                  pl.BlockSpec((d, wi.shape[1]), fixed),
                  pl.BlockSpec((1, d), fixed), pl.BlockSpec((1, d), fixed), pl.BlockSpec((1, LANES), fixed)],
        out_specs=(pl.BlockSpec((1, PAIRS, tm, LANES), lambda b, i: (b, 0, i, 0)),
                   pl.BlockSpec((1, PAIRS, tm, LANES), lambda b, i: (b, 0, i, 0)),
                   pl.BlockSpec((1, PAIRS, VT_ROWS, tm), lambda b, i: (b, 0, 0, i)),
                   pl.BlockSpec((1, IDX_PAIRS, tm, LANES), lambda b, i: (b, 0, i, 0)),
                   pl.BlockSpec((1, tm, LANES), lambda b, i: (b, i, 0)),
                   pl.BlockSpec((1, tm, LANES), lambda b, i: (b, i, 0))),
        compiler_params=_cp("parallel", "parallel"),
    )(h, wq, wk, wvt, wi, gq, gk, gki)


def _pair_split(x):
    lane = lax.broadcasted_iota(I32, x.shape, 1)
    zero = jnp.zeros_like(x)
    return jnp.concatenate([jnp.where(lane < HEAD, x, zero), jnp.where(lane >= HEAD, x, zero)], axis=0)


EXP_MASK = 0x7F800000


def _dsa_select_kernel(qi_ref, ki_ref, wi_ref, mask_ref, keys_ref, *, seq, topk):
    j = pl.program_id(1)
    lane = lax.broadcasted_iota(I32, (1, TQ), 1)
    limq = j * TQ + (lane // CHUNK + 1) * CHUNK
    nkt = (j * TQ + TQ + TK - 1) // TK
    wit = wi_ref[0].T
    w_pairs = [_pair_split(qi_ref[0, p]) for p in range(IDX_PAIRS)]
    kiota = lax.broadcasted_iota(I32, (TK, TQ), 0)

    def score_body(kt, carry):
        off = pl.multiple_of(kt * TK, TK)
        kit = ki_ref[0, pl.ds(off, TK), :]
        lgs = [_nt(kit, w_pairs[p]) for p in range(IDX_PAIRS)]
        s = jnp.zeros((TK, TQ), F32)
        for p in range(IDX_PAIRS):
            s = s + jnp.maximum(lgs[p][:, :TQ], 0.0) * wit[2 * p:2 * p + 1, :]
            s = s + jnp.maximum(lgs[p][:, TQ:], 0.0) * wit[2 * p + 1:2 * p + 2, :]
        s = jnp.where(off + kiota < limq, s, NEG_INF)
        bits = pltpu.bitcast(s, I32)
        bits = jnp.where((bits & EXP_MASK) == 0, 0, bits)
        keys_ref[pl.ds(off, TK), :] = jnp.where(bits < 0, bits ^ 0x7FFFFFFF, bits)
        return carry

    lax.fori_loop(0, nkt, score_body, 0)

    def count(pred):
        def body(i, acc):
            off = pl.multiple_of(i * TK, TK)
            hit = jnp.where(pred(keys_ref[pl.ds(off, TK), :], off), 1, 0)
            return acc + hit.reshape(TK // 8, 8, TQ).sum(axis=0)
        acc = lax.fori_loop(0, nkt, body, jnp.zeros((8, TQ), I32))
        return acc.sum(axis=0, keepdims=True)

    def search(_):
        def bit_body(t, tu):
            cand_u = tu | jnp.left_shift(jnp.int32(1), 31 - t)
            cand_s = cand_u ^ INT_MIN
            c = count(lambda kb, off: kb >= cand_s)
            return jnp.where(c >= topk, cand_u, tu)
        tu = lax.fori_loop(0, 32, bit_body, jnp.zeros((1, TQ), I32))
        return tu ^ INT_MIN

    ts = lax.cond(j * TQ + TQ > topk, search, lambda _: jnp.full((1, TQ), INT_MIN, I32), 0)

    n_ge = count(lambda kb, off: kb >= ts)

    @pl.when(jnp.max(n_ge) > topk)
    def _break_ties():
        r = topk - count(lambda kb, off: kb > ts)

        def bit_body(t, p):
            cand = p | jnp.left_shift(jnp.int32(1), (seq.bit_length() - 1) - t)
            c = count(lambda kb, off: (kb == ts) & (off + kiota < cand))
            return jnp.where(c < r, cand, p)
        p_last = lax.fori_loop(0, seq.bit_length(), bit_body, jnp.zeros((1, TQ), I32))

        def demote(kt, carry):
            off = pl.multiple_of(kt * TK, TK)
            kb = keys_ref[pl.ds(off, TK), :]
            keys_ref[pl.ds(off, TK), :] = jnp.where((kb == ts) & (off + kiota > p_last), ts - 1, kb)
            return carry
        lax.fori_loop(0, nkt, demote, 0)

    def out_body(kt, carry):
        off = pl.multiple_of(kt * TK, TK)

        @pl.when(kt < nkt)
        def _():
            kb = keys_ref[pl.ds(off, TK), :]
            sel = (kb >= ts) & (off + kiota < limq)
            mask_ref[0, 0, pl.ds(off, TK), :] = jnp.where(sel, 1, 0).astype(mask_ref.dtype)

        @pl.when(kt >= nkt)
        def _():
            mask_ref[0, 0, pl.ds(off, TK), :] = jnp.zeros((TK, TQ), mask_ref.dtype)
        return carry

    lax.fori_loop(0, seq // TK, out_body, 0)


def dsa_select(qi, ki, wi, topk):
    bsz, _, seq, _ = qi.shape
    nq = seq // TQ
    return pl.pallas_call(
        functools.partial(_dsa_select_kernel, seq=seq, topk=topk),
        out_shape=jax.ShapeDtypeStruct((bsz, nq, seq, TQ), jnp.int8),
        grid=(bsz, nq),
        in_specs=[pl.BlockSpec((1, IDX_PAIRS, TQ, LANES), lambda b, j: (b, 0, j, 0)),
                  pl.BlockSpec((1, seq, LANES), lambda b, j: (b, 0, 0)),
                  pl.BlockSpec((1, TQ, LANES), lambda b, j: (b, j, 0))],
        out_specs=pl.BlockSpec((1, 1, seq, TQ), lambda b, j: (b, j, 0, 0)),
        scratch_shapes=[pltpu.VMEM((seq, TQ), I32)],
        compiler_params=_cp("parallel", "parallel"),
    )(qi, ki, wi)


VT_ROWS = LANES + 16
S_CHUNK = 64
S_AHEAD = 3
S_SLOTS = S_AHEAD + 1


def _dsa_attn_kernel(jmap_ref, ktmap_ref, q_ref, k_ref, vt_ref, mask_ref, o_ref,
                     qm_ref, m_ref, l_ref, acc_ref, bias_ref, s_ref, p_ref):
    step = pl.program_id(1)
    j = jmap_ref[step]
    kt = ktmap_ref[step]
    last = (j * TQ + TQ - 1) // TK

    @pl.when(kt == 0)
    def _init():
        for p in range(PAIRS):
            qm_ref[p] = _pair_split(q_ref[0, p].astype(F32)).T.astype(BF16)
        m_ref[...] = jnp.full(m_ref.shape, -jnp.inf, F32)
        l_ref[...] = jnp.zeros(l_ref.shape, F32)
        acc_ref[...] = jnp.zeros(acc_ref.shape, F32)

    bias = (mask_ref[0, 0].astype(F32) - 1.0) * 1e30
    bias_ref[:, :TQ] = bias
    bias_ref[:, TQ:] = bias

    def scores(p):
        m_tile = None
        for hf in range(2):
            rows = slice(hf * (TK // 2), (hf + 1) * (TK // 2))
            s = _dot(k_ref[0, p, rows, :], qm_ref[p]) + bias_ref[rows, :]
            s_ref[p % S_SLOTS, rows, :] = s
            m_half = jnp.max(s, axis=0, keepdims=True)
            m_tile = m_half if m_tile is None else jnp.maximum(m_tile, m_half)
        return m_tile

    m_tiles = [scores(p) for p in range(S_AHEAD)]
    for p in range(PAIRS):
        slot = p % S_SLOTS
        m_old = m_ref[p]
        m_new = jnp.maximum(m_old, m_tiles[p])
        alpha = jnp.exp2(m_old - m_new)
        m_ref[p] = m_new
        if p + S_AHEAD < PAIRS:
            m_tiles.append(scores(p + S_AHEAD))
        for c in range(TK // S_CHUNK):
            rows = slice(c * S_CHUNK, (c + 1) * S_CHUNK)
            p_ref[slot, rows, :] = jnp.exp2(s_ref[slot, rows, :] - m_new).astype(BF16)
        pv = _dot(vt_ref[0, p], p_ref[slot])
        acc_ref[p] = acc_ref[p] * alpha + pv[:LANES]
        l_ref[p] = l_ref[p] * alpha + pv[LANES:LANES + 1]

    @pl.when(kt == last)
    def _finish():
        for p in range(PAIRS):
            a = acc_ref[p] / l_ref[p]
            ot = jnp.concatenate([a[0:HEAD, 0:TQ], a[HEAD:2 * HEAD, TQ:2 * TQ]], axis=0)
            o_ref[0, :, p * LANES:(p + 1) * LANES] = ot.T.astype(o_ref.dtype)


def dsa_attn(q, k, vt, mask):
    bsz, _, seq, _ = q.shape
    nq = seq // TQ
    visits = [(j, t) for j in range(nq) for t in range((j * TQ + TQ - 1) // TK + 1)]
    jmap = jnp.asarray([jt[0] for jt in visits], I32)
    ktmap = jnp.asarray([jt[1] for jt in visits], I32)
    grid_spec = pltpu.PrefetchScalarGridSpec(
        num_scalar_prefetch=2,
        grid=(bsz, len(visits)),
        in_specs=[pl.BlockSpec((1, PAIRS, TQ, LANES), lambda b, s, jm, km: (b, 0, jm[s], 0)),
                  pl.BlockSpec((1, PAIRS, TK, LANES), lambda b, s, jm, km: (b, 0, km[s], 0)),
                  pl.BlockSpec((1, PAIRS, VT_ROWS, TK), lambda b, s, jm, km: (b, 0, 0, km[s])),
                  pl.BlockSpec((1, 1, TK, TQ), lambda b, s, jm, km: (b, jm[s], km[s], 0))],
        out_specs=pl.BlockSpec((1, TQ, PAIRS * LANES), lambda b, s, jm, km: (b, jm[s], 0)),
        scratch_shapes=[pltpu.VMEM((PAIRS, LANES, 2 * TQ), BF16),
                        pltpu.VMEM((PAIRS, 1, 2 * TQ), F32),
                        pltpu.VMEM((PAIRS, 1, 2 * TQ), F32),
                        pltpu.VMEM((PAIRS, LANES, 2 * TQ), F32),
                        pltpu.VMEM((TK, 2 * TQ), F32),
                        pltpu.VMEM((S_SLOTS, TK, 2 * TQ), F32),
                        pltpu.VMEM((S_SLOTS, TK, 2 * TQ), BF16)])
    return pl.pallas_call(
        _dsa_attn_kernel,
        out_shape=jax.ShapeDtypeStruct((bsz, seq, PAIRS * LANES), BF16),
        grid_spec=grid_spec,
        compiler_params=_cp("parallel", "arbitrary"),
    )(jmap, ktmap, q, k, vt, mask)


def dsa_mixer(h, bsz, seq, w_in, q_gain, k_gain, kidx_gain):
    assert seq % TK == 0 and TK % TQ == 0
    topk = min(TOPK_MAX, seq // 4)
    q, k, vt, qi, ki, wi = dsa_proj(h, bsz, seq, w_in, q_gain, k_gain, kidx_gain)
    mask = dsa_select(qi, ki, wi, topk)
    return dsa_attn(q, k, vt, mask).reshape(bsz * seq, PAIRS * LANES)


RW_PREV = 8
RW_C = 64
RW_CH = 8


def _rwkv_proj_kernel(x_ref, xp_ref, gn_ref, mu_ref, wr_ref, wk_ref, wv_ref, w1_ref, w2_ref, a1_ref, a2_ref,
                      g1_ref, g2_ref, w0_ref, a0_ref, kk_ref, ka_ref,
                      r_out, lw_out, k_out, v_out, kk_out, b_out, g_out, *, tiles_per_seq):
    i = pl.program_id(0)
    gn = gn_ref[...]
    h = _rms(x_ref[...], gn)
    hp = _rms(xp_ref[...], gn)
    hp_last = jnp.where((i % tiles_per_seq) == 0, 0.0, hp[RW_PREV - 1:RW_PREV, :])
    rows = lax.broadcasted_iota(I32, h.shape, 0)
    dh = jnp.where(rows == 0, hp_last, pltpu.roll(h, 1, 0)) - h
    mu = mu_ref[...]
    xs = lambda n: (h + dh * mu[n:n + 1, :]).astype(BF16)
    r = _dot(xs(0), wr_ref[...])
    k = _dot(xs(1), wk_ref[...])
    v_out[...] = _dot(xs(2), wv_ref[...])
    wl = w0_ref[...] + _dot(jnp.tanh(_dot(xs(3), w1_ref[...])).astype(BF16), w2_ref[...])
    z = -wl
    w_log = -(jnp.maximum(z, 0.0) + jnp.log(1.0 + jnp.exp(-jnp.abs(z)))) - 0.5
    lw_out[...] = -jnp.exp(w_log)
    a = jax.nn.sigmoid(a0_ref[...] + _dot(_dot(xs(4), a1_ref[...]).astype(BF16), a2_ref[...]))
    g_out[...] = _dot(jax.nn.sigmoid(_dot(xs(5), g1_ref[...])).astype(BF16), g2_ref[...])
    r_out[...] = r
    k_out[...] = k * (1.0 + (a - 1.0) * ka_ref[...])
    kk = k * kk_ref[...]
    ones = _seg64_ones()
    for p in range(kk.shape[1] // SEG_W):
        sl = slice(p * SEG_W, (p + 1) * SEG_W)
        kp = kk[:, sl]
        kn = kp / jnp.maximum(jnp.sqrt(_seg64_sum(kp * kp, ones)), 1e-12)
        kk_out[:, sl] = kn
        b_out[:, sl] = kn * a[:, sl]


def rwkv_proj(x2, seq, gn, mu, w_rkv, w0, w1, w2, a0, a1, a2, g1, g2, k_k, k_a, tm=256):
    t, d = x2.shape
    pad_c = lambda w: jnp.pad(w, ((0, 0), (0, LANES - w.shape[1]))).astype(BF16)
    pad_r = lambda w: jnp.pad(w, ((0, LANES - w.shape[0]), (0, 0))).astype(BF16)
    row = lambda i: (i, 0)
    fixed = lambda i: (0, 0)
    vec = lambda a: a.reshape(1, d)
    full = lambda a: pl.BlockSpec(a.shape, fixed)
    args = [x2, x2, vec(gn), jnp.pad(mu, ((0, 2), (0, 0))),
            w_rkv[0].astype(BF16), w_rkv[1].astype(BF16), w_rkv[2].astype(BF16),
            pad_c(w1), pad_r(w2), pad_c(a1), pad_r(a2), g1.astype(BF16), g2.astype(BF16),
            vec(w0), vec(a0), vec(k_k), vec(k_a)]
    in_specs = [pl.BlockSpec((tm, d), row),
                pl.BlockSpec((RW_PREV, d), lambda i: (jnp.maximum(i * (tm // RW_PREV) - 1, 0), 0))]
    in_specs += [full(a) for a in args[2:]]
    return pl.pallas_call(
        functools.partial(_rwkv_proj_kernel, tiles_per_seq=seq // tm),
        out_shape=tuple(jax.ShapeDtypeStruct((t, d), F32) for _ in range(7)),
        grid=(t // tm,),
        in_specs=in_specs,
        out_specs=tuple(pl.BlockSpec((tm, d), row) for _ in range(7)),
        compiler_params=_cp("parallel"),
    )(*args)


def _bdot(a, b):
    return _dot(a.astype(BF16), b.astype(BF16))


def _rwkv_chunk_terms(r, lw, k2, v, kk, b, tick):
    c = RW_C
    cat = jnp.concatenate
    each = lambda f, *ls: [f(*a) for a in zip(*ls)]
    row = lax.broadcasted_iota(I32, (2 * c, 2 * c), 0)
    col = lax.broadcasted_iota(I32, (2 * c, 2 * c), 1)
    r_c = lax.broadcasted_iota(I32, (c, c), 0)
    c_c = lax.broadcasted_iota(I32, (c, c), 1)
    tri = jnp.where(c_c <= r_c, 1.0, 0.0).astype(BF16)

    def cum(lw_):
        hi = lw_.astype(BF16)
        rem = lw_ - hi.astype(F32)
        mid = rem.astype(BF16)
        lo = (rem - mid.astype(F32)).astype(BF16)
        return _dot(tri, hi) + _dot(tri, mid) + _dot(tri, lo)
    cs = each(cum, lw)
    e_pos = each(jnp.exp, cs)
    e_neg = each(lambda x: jnp.exp(-x), cs)
    mul = lambda x, y: x * y
    rt = each(mul, r, e_pos)
    kt = each(mul, k2, e_neg)
    bt = each(mul, b, e_neg)
    kkt = each(lambda x, s, l: x * jnp.exp(s - l), kk, cs, lw)
    g_end = each(lambda e: e[c - 1:c, :], e_pos)
    khat = each(mul, kt, g_end)
    bhat = each(mul, bt, g_end)
    lane = lax.broadcasted_iota(I32, (c, LANES), 1)
    t_row = lax.broadcasted_iota(I32, (c, LANES), 0)
    m0 = lane < HEAD
    h0 = lambda x: jnp.where(m0, x, 0.0)
    h1 = lambda x: jnp.where(m0, 0.0, x)
    g0 = each(lambda kq, rr, bb, kk_: _nt(cat([h0(kq), h0(rr)]).astype(BF16), cat([bb, kk_]).astype(BF16)),
              kkt, rt, bt, kt)
    tick()
    g1 = each(lambda kq, rr, bb, kk_: _nt(cat([h1(rr), h1(kq)]).astype(BF16), cat([kk_, bb]).astype(BF16)),
              kkt, rt, bt, kt)
    top, left = row < c, col < c
    m_ab0, m_ab1 = top & left & (col < row), ~top & ~left & (col < row)
    m_ak0, m_ak1 = top & ~left & (col - c < row), ~top & left & (col < row - c)
    a_b = each(lambda x0, x1: jnp.where(m_ab0, x0, 0.0) + jnp.where(m_ab1, x1, 0.0), g0, g1)
    a_k = each(lambda x0, x1: jnp.where(m_ak0, x0, 0.0) + jnp.where(m_ak1, x1, 0.0), g0, g1)
    eye = jnp.where(row == col, 1.0, 0.0)
    x = each(lambda a: eye - a, a_b)
    pw = each(lambda a: _bdot(a, a), a_b)
    tick()
    for it in range(5):
        x = each(lambda xx, pp: xx + _bdot(xx, pp), x, pw)
        tick()
        if it < 4:
            pw = each(lambda pp: _bdot(pp, pp), pw)
    v0, v1 = each(h0, v), each(h1, v)
    akv = each(lambda a, va, vb: _bdot(a, cat([vb, va])), a_k, v0, v1)
    tick()
    wu = each(lambda xx, kq, av: _bdot(xx, cat([cat([h0(kq), h1(kq)]), av], axis=1)), x, kkt, akv)
    incl = jnp.where(m0, lane, lane - HEAD) <= t_row
    zeros = jnp.zeros((c, LANES), F32)
    rhs0 = each(lambda w_, va: cat([-w_[0:c], cat([zeros, va], axis=1)]), wu, v0)
    rhs1 = each(lambda w_, vb: cat([cat([zeros, vb], axis=1), -w_[c:2 * c]]), wu, v1)
    o0 = each(lambda g, rh: _bdot(jnp.where(incl, g[c:2 * c], 0.0), rh), g0, rhs0)
    o1 = each(lambda g, rh: _bdot(jnp.where(incl, g[0:c], 0.0), rh), g1, rhs1)
    rq = each(lambda rr, a0, a1: rr + a0[:, :LANES] + a1[:, :LANES], rt, o0, o1)
    yin = each(lambda a0, a1: a0[:, LANES:] + a1[:, LANES:], o0, o1)
    pd = each(lambda bh, kh, ra, rb: _tn(cat([h0(bh), h0(kh), h1(kh), h1(bh)]).astype(BF16),
                                         cat([ra, rb]).astype(BF16)), bhat, khat, rhs0, rhs1)
    phi = each(lambda p_, ge: p_[:, :LANES] + jnp.where(row == col, jnp.broadcast_to(ge, (2 * c, LANES)), 0.0),
               pd, g_end)
    return [(a, b_, c_, p_[:, LANES:]) for a, b_, c_, p_ in zip(rq, yin, phi, pd)]


def _rwkv_scan_kernel(r_ref, lw_ref, k_ref, v_ref, kk_ref, b_ref, y_ref, h_ref, lhs_ref, yin_ref, dh_ref):
    s = pl.program_id(2)
    cur = s % 2
    prev = 1 - cur
    rows = [slice(ci * RW_C, (ci + 1) * RW_C) for ci in range(RW_CH)]

    @pl.when(s == 0)
    def _():
        lhs_ref[prev] = jnp.zeros(lhs_ref.shape[1:], lhs_ref.dtype)
        yin_ref[prev] = jnp.zeros(yin_ref.shape[1:], F32)
        dh_ref[prev] = jnp.zeros(dh_ref.shape[1:], F32)
        h_ref[...] = jnp.zeros(h_ref.shape, F32)

    state = [jnp.where(s <= 1, 0.0, h_ref[...])]
    links = iter(range(RW_CH))

    def recurrence_step():
        ci = next(links, None)
        if ci is None:
            return
        both = _dot(lhs_ref[prev, ci], state[0].astype(BF16))
        y_ref[0, rows[ci], :] = both[0:RW_C] + yin_ref[prev, ci]
        state[0] = both[RW_C:] + dh_ref[prev, ci]

    ld = lambda ref: [ref[0, sl, :] for sl in rows]
    terms = _rwkv_chunk_terms(ld(r_ref), ld(lw_ref), ld(k_ref), ld(v_ref), ld(kk_ref), ld(b_ref), recurrence_step)
    for _ in range(RW_CH):
        recurrence_step()
    h_ref[...] = state[0]
    for ci, (rq, yin, phi, dh) in enumerate(terms):
        lhs_ref[cur, ci] = jnp.concatenate([rq, phi]).astype(BF16)
        yin_ref[cur, ci] = yin
        dh_ref[cur, ci] = dh


def rwkv_scan(r, lw, k2, v, kk, b, bsz, seq):
    d = r.shape[-1]
    rows = RW_C * RW_CH
    nblk = seq // rows
    in_spec = pl.BlockSpec((1, rows, LANES), lambda bb, p, s: (bb, jnp.minimum(s, nblk - 1), p))
    out_spec = pl.BlockSpec((1, rows, LANES), lambda bb, p, s: (bb, jnp.maximum(s - 1, 0), p))
    shp = lambda a: a.reshape(bsz, seq, d)
    return pl.pallas_call(
        _rwkv_scan_kernel,
        out_shape=jax.ShapeDtypeStruct((bsz, seq, d), F32),
        grid=(bsz, d // LANES, nblk + 1),
        in_specs=[in_spec] * 6,
        out_specs=out_spec,
        scratch_shapes=[pltpu.VMEM((LANES, LANES), F32),
                        pltpu.VMEM((2, RW_CH, RW_C + LANES, LANES), BF16),
                        pltpu.VMEM((2, RW_CH, RW_C, LANES), F32),
                        pltpu.VMEM((2, RW_CH, LANES, LANES), F32)],
        compiler_params=_cp("parallel", "parallel", "arbitrary"),
    )(shp(r), shp(lw), shp(k2), shp(v), shp(kk), shp(b)).reshape(bsz * seq, d)


def _rwkv_post_kernel(y_ref, r_ref, k_ref, v_ref, g_ref, lng_ref, lnb_ref, rk_ref, w_ref, x_ref, gn_ref,
                      xo_ref, ho_ref, o_scr):
    ones = _seg64_ones()
    for p in range(y_ref.shape[1] // SEG_W):
        sl = slice(p * SEG_W, (p + 1) * SEG_W)
        y = y_ref[:, sl]
        dv = y - _seg64_sum(y, ones) * (1.0 / HEAD)
        var = _seg64_sum(dv * dv, ones) * (1.0 / HEAD)
        yn = dv * lax.rsqrt(var + GN_EPS) * lng_ref[:, sl] + lnb_ref[:, sl]
        bonus = _seg64_sum(r_ref[:, sl] * k_ref[:, sl] * rk_ref[:, sl], ones) * v_ref[:, sl]
        o_scr[:, sl] = ((yn + bonus) * g_ref[:, sl]).astype(o_scr.dtype)
    xn = x_ref[...] + _dot(o_scr[...], w_ref[...])
    xo_ref[...] = xn
    ho_ref[...] = _rms(xn, gn_ref[...]).astype(ho_ref.dtype)


def rwkv_post(y, r, k2, v, g, ln_g, ln_b, r_k, w_out, x2, gn, tm=256):
    t, d = x2.shape
    row = lambda i: (i, 0)
    fixed = lambda i: (0, 0)
    tile = pl.BlockSpec((tm, d), row)
    vecs = pl.BlockSpec((1, d), fixed)
    return pl.pallas_call(
        _rwkv_post_kernel,
        out_shape=(jax.ShapeDtypeStruct((t, d), F32), jax.ShapeDtypeStruct((t, d), BF16)),
        grid=(t // tm,),
        in_specs=[tile] * 5 + [vecs] * 3 + [pl.BlockSpec((d, d), fixed), tile, vecs],
        out_specs=(tile, tile),
        scratch_shapes=[pltpu.VMEM((tm, d), BF16)],
        compiler_params=_cp("parallel"),
    )(y, r, k2, v, g, ln_g.reshape(1, d), ln_b.reshape(1, d), r_k.reshape(1, d), w_out.astype(BF16), x2,
      gn.reshape(1, d))


S5_L = 16


def _split(a):
    hi = a.astype(BF16)
    return hi, (a - hi.astype(F32)).astype(BF16)


def _dot3(a, b_hi, b_lo):
    a_hi, a_lo = _split(a)
    return _dot(a_hi, b_hi) + _dot(a_lo, b_hi) + _dot(a_hi, b_lo)


def _s5_local_kernel(u_ref, grh_ref, grl_ref, gih_ref, gil_ref, xr_ref, xi_ref):
    u0, u1 = u_ref[0], u_ref[1]
    xr_ref[...] = _dot3(u0, grh_ref[0, 0], grl_ref[0, 0]) + _dot3(u1, grh_ref[0, 1], grl_ref[0, 1])
    xi_ref[...] = _dot3(u0, gih_ref[0, 0], gil_ref[0, 0]) + _dot3(u1, gih_ref[0, 1], gil_ref[0, 1])


def _s5_carry_kernel(er_ref, ei_ref, lr_ref, li_ref, pr_ref, pi_ref):
    lr = lr_ref[...]
    li = li_ref[...]

    def body(n, st):
        sr, si = st
        pr_ref[n] = sr
        pi_ref[n] = si
        return (lr * sr - li * si + er_ref[n], lr * si + li * sr + ei_ref[n])
    zero = jnp.zeros(lr.shape, F32)
    lax.fori_loop(0, er_ref.shape[0], body, (zero, zero))


def _s5_out_kernel(u_ref, pr_ref, pi_ref, kh_ref, kl_ref, erh_ref, erl_ref, eih_ref, eil_ref, y_ref):
    pr = pr_ref[...]
    pi = pi_ref[...]
    for i in range(2):
        y_ref[i] = (_dot3(u_ref[i], kh_ref[i], kl_ref[i]) + _dot3(pr, erh_ref[0, i], erl_ref[0, i])
                    + _dot3(pi, eih_ref[0, i], eil_ref[0, i]))


def s5_ssm(h3, a_re, a_im, log_step, b_re, b_im, c_re, c_im):
    bsz, seq, d = h3.shape
    ng, ns = a_re.shape
    gc = d // ng
    L = S5_L
    nc = seq // L
    n = bsz * nc
    step = jnp.exp(log_step.astype(F32))[:, None]
    lam = lax.complex(a_re.astype(F32), a_im.astype(F32))
    lam_bar = jnp.exp(lam * step)
    b_bar = ((lam_bar - 1.0) / lam)[..., None] * lax.complex(b_re.astype(F32), b_im.astype(F32))
    cc = lax.complex(c_re.astype(F32), c_im.astype(F32))
    pw = jnp.exp((lam * step)[:, None, :] * jnp.arange(L + 1, dtype=F32)[None, :, None])
    lag = jnp.arange(L)[None, :] - jnp.arange(L)[:, None]
    kfull = jnp.einsum('gcp,gstp,gpe->gsetc', cc, pw[:, jnp.clip(lag, 0, L)], b_bar)
    kmat = jnp.where((lag >= 0)[None, :, None, :, None], jnp.real(kfull), 0.0).reshape(ng, L * gc, L * gc)
    gfull = jnp.einsum('gsp,gpe->gsep', pw[:, L - 1 - jnp.arange(L)], b_bar).reshape(ng, L * gc, ns)
    efull = jnp.einsum('gcp,gtp->gptc', cc, pw[:, 1:]).reshape(ng, ns, L * gc)
    lam_l = pw[:, L]
    assert 2 * ns == LANES and ng % 16 == 0
    nq = ng // 2

    def cols(m):
        m4 = m.reshape(nq, 2, m.shape[1], ns)
        z = jnp.zeros_like(m4[:, 0])
        return jnp.stack([jnp.concatenate([m4[:, 0], z], -1), jnp.concatenate([z, m4[:, 1]], -1)], 1)

    def rows(m):
        m4 = m.reshape(nq, 2, ns, m.shape[2])
        z = jnp.zeros_like(m4[:, 0])
        return jnp.stack([jnp.concatenate([m4[:, 0], z], -2), jnp.concatenate([z, m4[:, 1]], -2)], 1)

    kh, kl = _split(kmat)
    grh, grl = _split(cols(jnp.real(gfull)))
    gih, gil = _split(cols(jnp.imag(gfull)))
    erh, erl = _split(rows(jnp.real(efull)))
    eih, eil = _split(rows(-jnp.imag(efull)))
    lr = jnp.real(lam_l).reshape(nq, LANES)
    li = jnp.imag(lam_l).reshape(nq, LANES)

    lw = L * gc
    u = h3.reshape(bsz, nc, L, ng, gc).transpose(3, 0, 1, 2, 4).reshape(ng, n, lw)
    pair3 = lambda q: (q, 0, 0)
    pair4 = lambda q: (q, 0, 0, 0)
    col = lambda q: (0, q)
    gspec = pl.BlockSpec((1, 2, lw, LANES), pair4)
    xr, xi = pl.pallas_call(
        _s5_local_kernel,
        out_shape=(jax.ShapeDtypeStruct((n, nq * LANES), F32),) * 2,
        grid=(nq,),
        in_specs=[pl.BlockSpec((2, n, lw), pair3), gspec, gspec, gspec, gspec],
        out_specs=(pl.BlockSpec((n, LANES), col),) * 2,
        compiler_params=_cp("parallel"),
    )(u, grh, grl, gih, gil)
    st_spec = pl.BlockSpec((nc, 8, LANES), lambda b, j: (b, j, 0))
    lam_spec = pl.BlockSpec((8, LANES), lambda b, j: (j, 0))
    pr, pi = pl.pallas_call(
        _s5_carry_kernel,
        out_shape=(jax.ShapeDtypeStruct((n, nq, LANES), F32),) * 2,
        grid=(bsz, nq // 8),
        in_specs=[st_spec, st_spec, lam_spec, lam_spec],
        out_specs=(st_spec, st_spec),
        compiler_params=_cp("parallel", "parallel"),
    )(xr.reshape(n, nq, LANES), xi.reshape(n, nq, LANES), lr, li)
    espec = pl.BlockSpec((1, 2, LANES, lw), pair4)
    kspec = pl.BlockSpec((2, lw, lw), pair3)
    y = pl.pallas_call(
        _s5_out_kernel,
        out_shape=jax.ShapeDtypeStruct((ng, n, lw), F32),
        grid=(nq,),
        in_specs=[pl.BlockSpec((2, n, lw), pair3), pl.BlockSpec((n, LANES), col), pl.BlockSpec((n, LANES), col),
                  kspec, kspec, espec, espec, espec, espec],
        out_specs=pl.BlockSpec((2, n, lw), pair3),
        compiler_params=_cp("parallel"),
    )(u, pr.reshape(n, nq * LANES), pi.reshape(n, nq * LANES), kh, kl, erh, erl, eih, eil)
    return y.reshape(ng, bsz, nc, L, gc).transpose(1, 2, 3, 0, 4).reshape(bsz * seq, d)


def _s5_glu_kernel(ys_ref, h_ref, d_ref, w_ref, b_ref, x_ref, gn_ref, xo_ref, ho_ref):
    y = ys_ref[...] + d_ref[...] * h_ref[...]
    gelu = 0.5 * y * (1.0 + jnp.tanh(math.sqrt(2.0 / math.pi) * (y + 0.044715 * (y * y * y))))
    z = _dot(gelu.astype(BF16), w_ref[...]) + b_ref[...]
    dm = x_ref.shape[1]
    xn = x_ref[...] + z[:, :dm] * jax.nn.sigmoid(z[:, dm:])
    xo_ref[...] = xn
    ho_ref[...] = _rms(xn, gn_ref[...]).astype(ho_ref.dtype)


def s5_glu(ys, h, d_skip, w_glu, b_glu, x2, gn, tm=256):
    t, d = x2.shape
    row = lambda i: (i, 0)
    fixed = lambda i: (0, 0)
    tile = pl.BlockSpec((tm, d), row)
    return pl.pallas_call(
        _s5_glu_kernel,
        out_shape=(jax.ShapeDtypeStruct((t, d), F32), jax.ShapeDtypeStruct((t, d), BF16)),
        grid=(t // tm,),
        in_specs=[tile, tile, pl.BlockSpec((1, d), fixed), pl.BlockSpec((d, 2 * d), fixed),
                  pl.BlockSpec((1, 2 * d), fixed), tile, pl.BlockSpec((1, d), fixed)],
        out_specs=(tile, tile),
        compiler_params=_cp("parallel"),
    )(ys, h, d_skip.reshape(1, d), w_glu.astype(BF16), b_glu.reshape(1, 2 * d), x2, gn.reshape(1, d))


def kernel(x, norm_mix, norm_ffn, dsa_w_in, dsa_q_norm, dsa_k_norm, dsa_kidx_norm, dsa_w_out, rwkv_mu, rwkv_w_rkv, rwkv_w0, rwkv_w1, rwkv_w2, rwkv_a0, rwkv_a1, rwkv_a2, rwkv_g1, rwkv_g2, rwkv_k_k, rwkv_k_a, rwkv_r_k, rwkv_ln_g, rwkv_ln_b, rwkv_w_out, s5_a_re, s5_a_im, s5_log_step, s5_b_re, s5_b_im, s5_c_re, s5_c_im, s5_d, s5_w_glu, s5_b_glu, ffn_w_up, ffn_conv_w, ffn_conv_b, ffn_w_down):
    bsz, seq, d = x.shape
    depth = norm_mix.shape[0]
    x2 = x.reshape(bsz * seq, d)
    for i in range(depth):
        kind, j = i % 3, i // 3
        if kind == 0:
            h = rmsnorm(x2, norm_mix[i], BF16)
            o = dsa_mixer(h, bsz, seq, dsa_w_in[j], dsa_q_norm[j], dsa_k_norm[j], dsa_kidx_norm[j])
            x2, h = mm_res_norm(o, dsa_w_out[j].astype(BF16), x2, norm_ffn[i], BF16)
        elif kind == 1:
            r, lw, k2, v, kk, b, g = rwkv_proj(x2, seq, norm_mix[i], rwkv_mu[j], rwkv_w_rkv[j], rwkv_w0[j],
                                               rwkv_w1[j], rwkv_w2[j], rwkv_a0[j], rwkv_a1[j], rwkv_a2[j],
                                               rwkv_g1[j], rwkv_g2[j], rwkv_k_k[j], rwkv_k_a[j])
            y = rwkv_scan(r, lw, k2, v, kk, b, bsz, seq)
            x2, h = rwkv_post(y, r, k2, v, g, rwkv_ln_g[j], rwkv_ln_b[j], rwkv_r_k[j].reshape(d), rwkv_w_out[j],
                              x2, norm_ffn[i])
        else:
            hf = rmsnorm(x2, norm_mix[i], F32)
            ys = s5_ssm(hf.reshape(bsz, seq, d), s5_a_re[j], s5_a_im[j], s5_log_step[j], s5_b_re[j], s5_b_im[j],
                        s5_c_re[j], s5_c_im[j])
            x2, h = s5_glu(ys, hf, s5_d[j], s5_w_glu[j], s5_b_glu[j], x2, norm_ffn[i])
        act = ffn_up(h, ffn_w_up[i].astype(BF16), ffn_conv_w[i], ffn_conv_b[i], seq)
        x2, _ = mm_res_norm(act, ffn_w_down[i].astype(BF16), x2, None, None)
    return x2.reshape(bsz, seq, d)
```

```python
import functools
import math

import jax
import jax.numpy as jnp
from jax import lax
from jax.experimental import pallas as pl
from jax.experimental.pallas import tpu as pltpu

F32 = jnp.float32
BF16 = jnp.bfloat16
I32 = jnp.int32

EPS = 1e-6
NEG_INF = -1e30
LANES = 128
HEAD = 64
CHUNK = 64
TOPK_MAX = 256
GN_EPS = 64e-5
INT_MIN = -(2 ** 31)

VMEM_LIMIT = 56 * 1024 * 1024


def _cp(*sem):
    return pltpu.CompilerParams(dimension_semantics=sem, vmem_limit_bytes=VMEM_LIMIT)


def _nt(a, b):
    return lax.dot_general(a, b, (((1,), (1,)), ((), ())), preferred_element_type=F32)


def _tn(a, b):
    return lax.dot_general(a, b, (((0,), (0,)), ((), ())), preferred_element_type=F32)


def _dot(a, b):
    return jnp.dot(a, b, preferred_element_type=F32)


def _rms(x, g):
    ms = jnp.mean(x * x, axis=-1, keepdims=True)
    return x * lax.rsqrt(ms + EPS) * g


SEG_W = 2 * LANES


def _seg64_ones():
    r = lax.broadcasted_iota(I32, (SEG_W, SEG_W), 0) // HEAD
    c = lax.broadcasted_iota(I32, (SEG_W, SEG_W), 1) // HEAD
    return jnp.where(r == c, 1.0, 0.0).astype(BF16)


def _seg64_sum(x, ones):
    hi = x.astype(BF16)
    lo = (x - hi.astype(F32)).astype(BF16)
    return _dot(hi, ones) + _dot(lo, ones)


def _norm_kernel(x_ref, g_ref, o_ref):
    o_ref[...] = _rms(x_ref[...], g_ref[...]).astype(o_ref.dtype)


def rmsnorm(x2, g, out_dtype, tm=1024):
    t, d = x2.shape
    return pl.pallas_call(
        _norm_kernel,
        out_shape=jax.ShapeDtypeStruct((t, d), out_dtype),
        grid=(t // tm,),
        in_specs=[pl.BlockSpec((tm, d), lambda i: (i, 0)),
                  pl.BlockSpec((1, d), lambda i: (0, 0))],
        out_specs=pl.BlockSpec((tm, d), lambda i: (i, 0)),
        compiler_params=_cp("parallel"),
    )(x2, g.reshape(1, d))


def _mm_res_norm_kernel(a_ref, w_ref, x_ref, g_ref, xo_ref, ho_ref):
    xn = x_ref[...] + _dot(a_ref[...], w_ref[...])
    xo_ref[...] = xn
    ho_ref[...] = _rms(xn, g_ref[...]).astype(ho_ref.dtype)


def _mm_res_kernel(a_ref, w_ref, x_ref, xo_ref):
    xo_ref[...] = x_ref[...] + _dot(a_ref[...], w_ref[...])


def mm_res_norm(a, w, x2, g, h_dtype, tm=512):
    t, k = a.shape
    d = w.shape[1]
    row = lambda i: (i, 0)
    fixed = lambda i: (0, 0)
    in_specs = [pl.BlockSpec((tm, k), row), pl.BlockSpec((k, d), fixed), pl.BlockSpec((tm, d), row)]
    if g is None:
        return pl.pallas_call(
            _mm_res_kernel,
            out_shape=jax.ShapeDtypeStruct((t, d), F32),
            grid=(t // tm,), in_specs=in_specs, out_specs=pl.BlockSpec((tm, d), row),
            compiler_params=_cp("parallel"),
        )(a, w, x2), None
    return pl.pallas_call(
        _mm_res_norm_kernel,
        out_shape=(jax.ShapeDtypeStruct((t, d), F32), jax.ShapeDtypeStruct((t, d), h_dtype)),
        grid=(t // tm,),
        in_specs=in_specs + [pl.BlockSpec((1, d), fixed)],
        out_specs=(pl.BlockSpec((tm, d), row), pl.BlockSpec((tm, d), row)),
        compiler_params=_cp("parallel"),
    )(a, w, x2, g.reshape(1, d))


PREV_ROWS = 16


FFN_TN = 256


def _ffn_up_kernel(h_ref, hp_ref, w_ref, cw_ref, cb_ref, o_ref, *, tiles_per_seq):
    i = pl.program_id(0)
    h = h_ref[...]
    hp = hp_ref[...]
    f = o_ref.shape[1]
    seq_start = (i % tiles_per_seq) == 0
    head = PREV_ROWS
    rows = lax.broadcasted_iota(I32, (head, FFN_TN), 0)

    def project(c):
        out = []
        for base in (0, f):
            cols = slice(base + c * FFN_TN, base + (c + 1) * FFN_TN)
            w = w_ref[:, cols]
            out.append((_dot(h, w), jnp.where(seq_start, 0.0, _dot(hp, w)), cols))
        return out

    def conv(u, up, cols):
        cw = cw_ref[:, cols]
        cb = cb_ref[:, cols]
        mix = lambda u2, u1, u0: cw[0:1, :] * u2 + cw[1:2, :] * u1 + cw[2:3, :] * u0 + cb
        full = mix(pltpu.roll(u, 2, 0), pltpu.roll(u, 1, 0), u)
        uh = u[0:head]
        last = up[PREV_ROWS - 1:PREV_ROWS, :]
        u1 = jnp.where(rows == 0, last, pltpu.roll(uh, 1, 0))
        u2 = jnp.where(rows == 0, up[PREV_ROWS - 2:PREV_ROWS - 1, :],
                       jnp.where(rows == 1, last, pltpu.roll(uh, 2, 0)))
        return full, mix(u2, u1, uh)

    n_chunks = f // FFN_TN
    nxt = project(0)
    for c in range(n_chunks):
        (ug, upg, cg), (uv, upv, cv) = nxt
        if c + 1 < n_chunks:
            nxt = project(c + 1)
        gate, gate_h = conv(ug, upg, cg)
        val, val_h = conv(uv, upv, cv)
        o_ref[:, cg] = (gate * jax.nn.sigmoid(gate) * val).astype(o_ref.dtype)
        o_ref[0:head, cg] = (gate_h * jax.nn.sigmoid(gate_h) * val_h).astype(o_ref.dtype)


def ffn_up(h, w_up, conv_w, conv_b, seq, tm=512):
    t, d = h.shape
    f = w_up.shape[1] // 2
    assert f % FFN_TN == 0 and seq % tm == 0
    cw = jnp.zeros((8, 2 * f), F32).at[:conv_w.shape[0]].set(conv_w)
    cb = conv_b.reshape(1, 2 * f)
    fixed = lambda i: (0, 0)
    return pl.pallas_call(
        functools.partial(_ffn_up_kernel, tiles_per_seq=seq // tm),
        out_shape=jax.ShapeDtypeStruct((t, f), BF16),
        grid=(t // tm,),
        in_specs=[pl.BlockSpec((tm, d), lambda i: (i, 0)),
                  pl.BlockSpec((PREV_ROWS, d), lambda i: (jnp.maximum(i * (tm // PREV_ROWS) - 1, 0), 0)),
                  pl.BlockSpec((d, 2 * f), fixed),
                  pl.BlockSpec((8, 2 * f), fixed),
                  pl.BlockSpec((1, 2 * f), fixed)],
        out_specs=pl.BlockSpec((tm, f), lambda i: (i, 0)),
        compiler_params=_cp("parallel"),
    )(h, h, w_up, cw, cb)


PAIRS = 8
IDX_PAIRS = 4
TQ = 128
TK = 512


def _dsa_proj_kernel(h_ref, wq_ref, wk_ref, wvt_ref, wi_ref, gq_ref, gk_ref, gki_ref,
                     q_ref, k_ref, vt_ref, qi_ref, ki_ref, wi_out_ref):
    h = h_ref[...]
    tm = h.shape[0]

    ones = _seg64_ones()

    def head_norm(y, g_ref, o_ref):
        for p2 in range(PAIRS // 2):
            sl = slice(p2 * SEG_W, (p2 + 1) * SEG_W)
            yp = y[:, sl]
            ms = _seg64_sum(yp * yp, ones) * (1.0 / HEAD)
            yn = (yp * lax.rsqrt(ms + EPS) * g_ref[:, sl]).astype(o_ref.dtype)
            o_ref[0, 2 * p2] = yn[:, :LANES]
            o_ref[0, 2 * p2 + 1] = yn[:, LANES:]

    head_norm(_dot(h, wq_ref[...]), gq_ref, q_ref)
    head_norm(_dot(h, wk_ref[...]), gk_ref, k_ref)
    vt = _nt(wvt_ref[...], h)
    vt_ref[0, :, :LANES, :] = vt.reshape(PAIRS, LANES, tm).astype(vt_ref.dtype)
    vt_ref[0, :, LANES:, :] = jnp.ones((PAIRS, vt_ref.shape[2] - LANES, tm), vt_ref.dtype)
    idx = _dot(h, wi_ref[...])
    for p in range(IDX_PAIRS):
        qi_ref[0, p] = (idx[:, p * LANES:(p + 1) * LANES] * (HEAD ** -0.5)).astype(qi_ref.dtype)
    kw = idx[:, IDX_PAIRS * LANES:]
    ms = _seg64_sum(kw * kw, ones)[:, :LANES] * (1.0 / HEAD)
    ki_ref[0] = (kw[:, :LANES] * lax.rsqrt(ms + EPS) * gki_ref[...]).astype(ki_ref.dtype)
    wi_out_ref[0] = idx[:, (IDX_PAIRS + 1) * LANES:] * (2 * IDX_PAIRS) ** -0.5


def dsa_proj(h, bsz, seq, w_in, q_gain, k_gain, kidx_gain, tm=256):
    t, d = h.shape
    n_idx = 2 * IDX_PAIRS
    wq = w_in[:, :d].astype(BF16)
    wk = w_in[:, d:2 * d].astype(BF16)
    wvt = w_in[:, 2 * d:3 * d].T.astype(BF16)
    o = 3 * d
    w_qi = w_in[:, o:o + n_idx * HEAD]
    w_ki = w_in[:, o + n_idx * HEAD:o + n_idx * HEAD + HEAD]
    w_wi = w_in[:, o + n_idx * HEAD + HEAD:]
    w_wi = jnp.pad(w_wi, ((0, 0), (0, LANES - n_idx)))
    wi = jnp.concatenate([w_qi, w_ki, w_ki, w_wi], axis=1).astype(BF16)
    gq = (jnp.tile(q_gain, d // HEAD) * (HEAD ** -0.5 * math.log2(math.e))).reshape(1, d)
    gk = jnp.tile(k_gain, d // HEAD).reshape(1, d)
    gki = jnp.tile(kidx_gain, 2).reshape(1, LANES)
    nb = seq // tm
    fixed = lambda b, i: (0, 0)
    return pl.pallas_call(
        _dsa_proj_kernel,
        out_shape=(jax.ShapeDtypeStruct((bsz, PAIRS, seq, LANES), BF16),
                   jax.ShapeDtypeStruct((bsz, PAIRS, seq, LANES), BF16),
                   jax.ShapeDtypeStruct((bsz, PAIRS, VT_ROWS, seq), BF16),
                   jax.ShapeDtypeStruct((bsz, IDX_PAIRS, seq, LANES), BF16),
                   jax.ShapeDtypeStruct((bsz, seq, LANES), BF16),
                   jax.ShapeDtypeStruct((bsz, seq, LANES), F32)),
        grid=(bsz, nb),
        in_specs=[pl.BlockSpec((tm, d), lambda b, i: (b * nb + i, 0)),
                  pl.BlockSpec((d, d), fixed), pl.BlockSpec((d, d), fixed), pl.BlockSpec((d, d), fixed),
                  pl.BlockSpec((d, wi.shape[1]), fixed),
                  pl.BlockSpec((1, d), fixed), pl.BlockSpec((1, d), fixed), pl.BlockSpec((1, LANES), fixed)],
        out_specs=(pl.BlockSpec((1, PAIRS, tm, LANES), lambda b, i: (b, 0, i, 0)),
                   pl.BlockSpec((1, PAIRS, tm, LANES), lambda b, i: (b, 0, i, 0)),
                   pl.BlockSpec((1, PAIRS, VT_ROWS, tm), lambda b, i: (b, 0, 0, i)),
                   pl.BlockSpec((1, IDX_PAIRS, tm, LANES), lambda b, i: (b, 0, i, 0)),
                   pl.BlockSpec((1, tm, LANES), lambda b, i: (b, i, 0)),
                   pl.BlockSpec((1, tm, LANES), lambda b, i: (b, i, 0))),
        compiler_params=_cp("parallel", "parallel"),
    )(h, wq, wk, wvt, wi, gq, gk, gki)


def _pair_split(x):
    lane = lax.broadcasted_iota(I32, x.shape, 1)
    zero = jnp.zeros_like(x)
    return jnp.concatenate([jnp.where(lane < HEAD, x, zero), jnp.where(lane >= HEAD, x, zero)], axis=0)


EXP_MASK = 0x7F800000


def _dsa_select_kernel(qi_ref, ki_ref, wi_ref, mask_ref, keys_ref, *, seq, topk):
    j = pl.program_id(1)
    lane = lax.broadcasted_iota(I32, (1, TQ), 1)
    limq = j * TQ + (lane // CHUNK + 1) * CHUNK
    nkt = (j * TQ + TQ + TK - 1) // TK
    wit = wi_ref[0].T
    w_pairs = [_pair_split(qi_ref[0, p]) for p in range(IDX_PAIRS)]
    kiota = lax.broadcasted_iota(I32, (TK, TQ), 0)

    def score_body(kt, carry):
        off = pl.multiple_of(kt * TK, TK)
        kit = ki_ref[0, pl.ds(off, TK), :]
        lgs = [_nt(kit, w_pairs[p]) for p in range(IDX_PAIRS)]
        s = jnp.zeros((TK, TQ), F32)
        for p in range(IDX_PAIRS):
            s = s + jnp.maximum(lgs[p][:, :TQ], 0.0) * wit[2 * p:2 * p + 1, :]
            s = s + jnp.maximum(lgs[p][:, TQ:], 0.0) * wit[2 * p + 1:2 * p + 2, :]
        s = jnp.where(off + kiota < limq, s, NEG_INF)
        bits = pltpu.bitcast(s, I32)
        bits = jnp.where((bits & EXP_MASK) == 0, 0, bits)
        keys_ref[pl.ds(off, TK), :] = jnp.where(bits < 0, bits ^ 0x7FFFFFFF, bits)
        return carry

    lax.fori_loop(0, nkt, score_body, 0)

    def count(pred):
        def body(i, acc):
            off = pl.multiple_of(i * TK, TK)
            hit = jnp.where(pred(keys_ref[pl.ds(off, TK), :], off), 1, 0)
            return acc + hit.reshape(TK // 8, 8, TQ).sum(axis=0)
        acc = lax.fori_loop(0, nkt, body, jnp.zeros((8, TQ), I32))
        return acc.sum(axis=0, keepdims=True)

    def search(_):
        def bit_body(t, tu):
            cand_u = tu | jnp.left_shift(jnp.int32(1), 31 - t)
            cand_s = cand_u ^ INT_MIN
            c = count(lambda kb, off: kb >= cand_s)
            return jnp.where(c >= topk, cand_u, tu)
        tu = lax.fori_loop(0, 32, bit_body, jnp.zeros((1, TQ), I32))
        return tu ^ INT_MIN

    ts = lax.cond(j * TQ + TQ > topk, search, lambda _: jnp.full((1, TQ), INT_MIN, I32), 0)

    n_ge = count(lambda kb, off: kb >= ts)

    @pl.when(jnp.max(n_ge) > topk)
    def _break_ties():
        r = topk - count(lambda kb, off: kb > ts)

        def bit_body(t, p):
            cand = p | jnp.left_shift(jnp.int32(1), (seq.bit_length() - 1) - t)
            c = count(lambda kb, off: (kb == ts) & (off + kiota < cand))
            return jnp.where(c < r, cand, p)
        p_last = lax.fori_loop(0, seq.bit_length(), bit_body, jnp.zeros((1, TQ), I32))

        def demote(kt, carry):
            off = pl.multiple_of(kt * TK, TK)
            kb = keys_ref[pl.ds(off, TK), :]
            keys_ref[pl.ds(off, TK), :] = jnp.where((kb == ts) & (off + kiota > p_last), ts - 1, kb)
            return carry
        lax.fori_loop(0, nkt, demote, 0)

    def out_body(kt, carry):
        off = pl.multiple_of(kt * TK, TK)

        @pl.when(kt < nkt)
        def _():
            kb = keys_ref[pl.ds(off, TK), :]
            sel = (kb >= ts) & (off + kiota < limq)
            mask_ref[0, 0, pl.ds(off, TK), :] = jnp.where(sel, 1, 0).astype(mask_ref.dtype)

        @pl.when(kt >= nkt)
        def _():
            mask_ref[0, 0, pl.ds(off, TK), :] = jnp.zeros((TK, TQ), mask_ref.dtype)
        return carry

    lax.fori_loop(0, seq // TK, out_body, 0)


def dsa_select(qi, ki, wi, topk):
    bsz, _, seq, _ = qi.shape
    nq = seq // TQ
    return pl.pallas_call(
        functools.partial(_dsa_select_kernel, seq=seq, topk=topk),
        out_shape=jax.ShapeDtypeStruct((bsz, nq, seq, TQ), jnp.int8),
        grid=(bsz, nq),
        in_specs=[pl.BlockSpec((1, IDX_PAIRS, TQ, LANES), lambda b, j: (b, 0, j, 0)),
                  pl.BlockSpec((1, seq, LANES), lambda b, j: (b, 0, 0)),
                  pl.BlockSpec((1, TQ, LANES), lambda b, j: (b, j, 0))],
        out_specs=pl.BlockSpec((1, 1, seq, TQ), lambda b, j: (b, j, 0, 0)),
        scratch_shapes=[pltpu.VMEM((seq, TQ), I32)],
        compiler_params=_cp("parallel", "parallel"),
    )(qi, ki, wi)


VT_ROWS = LANES + 16
S_CHUNK = 64
S_AHEAD = 3
S_SLOTS = S_AHEAD + 1


def _dsa_attn_kernel(jmap_ref, ktmap_ref, q_ref, k_ref, vt_ref, mask_ref, o_ref,
                     qm_ref, m_ref, l_ref, acc_ref, bias_ref, s_ref, p_ref):
    step = pl.program_id(1)
    j = jmap_ref[step]
    kt = ktmap_ref[step]
    last = (j * TQ + TQ - 1) // TK

    @pl.when(kt == 0)
    def _init():
        for p in range(PAIRS):
            qm_ref[p] = _pair_split(q_ref[0, p].astype(F32)).T.astype(BF16)
        m_ref[...] = jnp.full(m_ref.shape, -jnp.inf, F32)
        l_ref[...] = jnp.zeros(l_ref.shape, F32)
        acc_ref[...] = jnp.zeros(acc_ref.shape, F32)

    bias = (mask_ref[0, 0].astype(F32) - 1.0) * 1e30
    bias_ref[:, :TQ] = bias
    bias_ref[:, TQ:] = bias

    def scores(p):
        m_tile = None
        for hf in range(2):
            rows = slice(hf * (TK // 2), (hf + 1) * (TK // 2))
            s = _dot(k_ref[0, p, rows, :], qm_ref[p]) + bias_ref[rows, :]
            s_ref[p % S_SLOTS, rows, :] = s
            m_half = jnp.max(s, axis=0, keepdims=True)
            m_tile = m_half if m_tile is None else jnp.maximum(m_tile, m_half)
        return m_tile

    m_tiles = [scores(p) for p in range(S_AHEAD)]
    for p in range(PAIRS):
        slot = p % S_SLOTS
        m_old = m_ref[p]
        m_new = jnp.maximum(m_old, m_tiles[p])
        alpha = jnp.exp2(m_old - m_new)
        m_ref[p] = m_new
        if p + S_AHEAD < PAIRS:
            m_tiles.append(scores(p + S_AHEAD))
        for c in range(TK // S_CHUNK):
            rows = slice(c * S_CHUNK, (c + 1) * S_CHUNK)
            p_ref[slot, rows, :] = jnp.exp2(s_ref[slot, rows, :] - m_new).astype(BF16)
        pv = _dot(vt_ref[0, p], p_ref[slot])
        acc_ref[p] = acc_ref[p] * alpha + pv[:LANES]
        l_ref[p] = l_ref[p] * alpha + pv[LANES:LANES + 1]

    @pl.when(kt == last)
    def _finish():
        for p in range(PAIRS):
            a = acc_ref[p] / l_ref[p]
            ot = jnp.concatenate([a[0:HEAD, 0:TQ], a[HEAD:2 * HEAD, TQ:2 * TQ]], axis=0)
            o_ref[0, :, p * LANES:(p + 1) * LANES] = ot.T.astype(o_ref.dtype)


def dsa_attn(q, k, vt, mask):
    bsz, _, seq, _ = q.shape
    nq = seq // TQ
    visits = [(j, t) for j in range(nq) for t in range((j * TQ + TQ - 1) // TK + 1)]
    jmap = jnp.asarray([jt[0] for jt in visits], I32)
    ktmap = jnp.asarray([jt[1] for jt in visits], I32)
    grid_spec = pltpu.PrefetchScalarGridSpec(
        num_scalar_prefetch=2,
        grid=(bsz, len(visits)),
        in_specs=[pl.BlockSpec((1, PAIRS, TQ, LANES), lambda b, s, jm, km: (b, 0, jm[s], 0)),
                  pl.BlockSpec((1, PAIRS, TK, LANES), lambda b, s, jm, km: (b, 0, km[s], 0)),
                  pl.BlockSpec((1, PAIRS, VT_ROWS, TK), lambda b, s, jm, km: (b, 0, 0, km[s])),
                  pl.BlockSpec((1, 1, TK, TQ), lambda b, s, jm, km: (b, jm[s], km[s], 0))],
        out_specs=pl.BlockSpec((1, TQ, PAIRS * LANES), lambda b, s, jm, km: (b, jm[s], 0)),
        scratch_shapes=[pltpu.VMEM((PAIRS, LANES, 2 * TQ), BF16),
                        pltpu.VMEM((PAIRS, 1, 2 * TQ), F32),
                        pltpu.VMEM((PAIRS, 1, 2 * TQ), F32),
                        pltpu.VMEM((PAIRS, LANES, 2 * TQ), F32),
                        pltpu.VMEM((TK, 2 * TQ), F32),
                        pltpu.VMEM((S_SLOTS, TK, 2 * TQ), F32),
                        pltpu.VMEM((S_SLOTS, TK, 2 * TQ), BF16)])
    return pl.pallas_call(
        _dsa_attn_kernel,
        out_shape=jax.ShapeDtypeStruct((bsz, seq, PAIRS * LANES), BF16),
        grid_spec=grid_spec,
        compiler_params=_cp("parallel", "arbitrary"),
    )(jmap, ktmap, q, k, vt, mask)


def dsa_mixer(h, bsz, seq, w_in, q_gain, k_gain, kidx_gain):
    assert seq % TK == 0 and TK % TQ == 0
    topk = min(TOPK_MAX, seq // 4)
    q, k, vt, qi, ki, wi = dsa_proj(h, bsz, seq, w_in, q_gain, k_gain, kidx_gain)
    mask = dsa_select(qi, ki, wi, topk)
    return dsa_attn(q, k, vt, mask).reshape(bsz * seq, PAIRS * LANES)


RW_PREV = 8
RW_C = 64
RW_CH = 8


def _rwkv_proj_kernel(x_ref, xp_ref, gn_ref, mu_ref, wr_ref, wk_ref, wv_ref, w1_ref, w2_ref, a1_ref, a2_ref,
                      g1_ref, g2_ref, w0_ref, a0_ref, kk_ref, ka_ref,
                      r_out, lw_out, k_out, v_out, kk_out, b_out, g_out, *, tiles_per_seq):
    i = pl.program_id(0)
    gn = gn_ref[...]
    h = _rms(x_ref[...], gn)
    hp = _rms(xp_ref[...], gn)
    hp_last = jnp.where((i % tiles_per_seq) == 0, 0.0, hp[RW_PREV - 1:RW_PREV, :])
    rows = lax.broadcasted_iota(I32, h.shape, 0)
    dh = jnp.where(rows == 0, hp_last, pltpu.roll(h, 1, 0)) - h
    mu = mu_ref[...]
    xs = lambda n: (h + dh * mu[n:n + 1, :]).astype(BF16)
    r = _dot(xs(0), wr_ref[...])
    k = _dot(xs(1), wk_ref[...])
    v_out[...] = _dot(xs(2), wv_ref[...])
    wl = w0_ref[...] + _dot(jnp.tanh(_dot(xs(3), w1_ref[...])).astype(BF16), w2_ref[...])
    z = -wl
    w_log = -(jnp.maximum(z, 0.0) + jnp.log(1.0 + jnp.exp(-jnp.abs(z)))) - 0.5
    lw_out[...] = -jnp.exp(w_log)
    a = jax.nn.sigmoid(a0_ref[...] + _dot(_dot(xs(4), a1_ref[...]).astype(BF16), a2_ref[...]))
    g_out[...] = _dot(jax.nn.sigmoid(_dot(xs(5), g1_ref[...])).astype(BF16), g2_ref[...])
    r_out[...] = r
    k_out[...] = k * (1.0 + (a - 1.0) * ka_ref[...])
    kk = k * kk_ref[...]
    ones = _seg64_ones()
    for p in range(kk.shape[1] // SEG_W):
        sl = slice(p * SEG_W, (p + 1) * SEG_W)
        kp = kk[:, sl]
        kn = kp / jnp.maximum(jnp.sqrt(_seg64_sum(kp * kp, ones)), 1e-12)
        kk_out[:, sl] = kn
        b_out[:, sl] = kn * a[:, sl]


def rwkv_proj(x2, seq, gn, mu, w_rkv, w0, w1, w2, a0, a1, a2, g1, g2, k_k, k_a, tm=256):
    t, d = x2.shape
    pad_c = lambda w: jnp.pad(w, ((0, 0), (0, LANES - w.shape[1]))).astype(BF16)
    pad_r = lambda w: jnp.pad(w, ((0, LANES - w.shape[0]), (0, 0))).astype(BF16)
    row = lambda i: (i, 0)
    fixed = lambda i: (0, 0)
    vec = lambda a: a.reshape(1, d)
    full = lambda a: pl.BlockSpec(a.shape, fixed)
    args = [x2, x2, vec(gn), jnp.pad(mu, ((0, 2), (0, 0))),
            w_rkv[0].astype(BF16), w_rkv[1].astype(BF16), w_rkv[2].astype(BF16),
            pad_c(w1), pad_r(w2), pad_c(a1), pad_r(a2), g1.astype(BF16), g2.astype(BF16),
            vec(w0), vec(a0), vec(k_k), vec(k_a)]
    in_specs = [pl.BlockSpec((tm, d), row),
                pl.BlockSpec((RW_PREV, d), lambda i: (jnp.maximum(i * (tm // RW_PREV) - 1, 0), 0))]
    in_specs += [full(a) for a in args[2:]]
    return pl.pallas_call(
        functools.partial(_rwkv_proj_kernel, tiles_per_seq=seq // tm),
        out_shape=tuple(jax.ShapeDtypeStruct((t, d), F32) for _ in range(7)),
        grid=(t // tm,),
        in_specs=in_specs,
        out_specs=tuple(pl.BlockSpec((tm, d), row) for _ in range(7)),
        compiler_params=_cp("parallel"),
    )(*args)


def _bdot(a, b):
    return _dot(a.astype(BF16), b.astype(BF16))


def _rwkv_chunk_terms(r, lw, k2, v, kk, b, tick):
    c = RW_C
    cat = jnp.concatenate
    each = lambda f, *ls: [f(*a) for a in zip(*ls)]
    row = lax.broadcasted_iota(I32, (2 * c, 2 * c), 0)
    col = lax.broadcasted_iota(I32, (2 * c, 2 * c), 1)
    r_c = lax.broadcasted_iota(I32, (c, c), 0)
    c_c = lax.broadcasted_iota(I32, (c, c), 1)
    tri = jnp.where(c_c <= r_c, 1.0, 0.0).astype(BF16)

    def cum(lw_):
        hi = lw_.astype(BF16)
        rem = lw_ - hi.astype(F32)
        mid = rem.astype(BF16)
        lo = (rem - mid.astype(F32)).astype(BF16)
        return _dot(tri, hi) + _dot(tri, mid) + _dot(tri, lo)
    cs = each(cum, lw)
    e_pos = each(jnp.exp, cs)
    e_neg = each(lambda x: jnp.exp(-x), cs)
    mul = lambda x, y: x * y
    rt = each(mul, r, e_pos)
    kt = each(mul, k2, e_neg)
    bt = each(mul, b, e_neg)
    kkt = each(lambda x, s, l: x * jnp.exp(s - l), kk, cs, lw)
    g_end = each(lambda e: e[c - 1:c, :], e_pos)
    khat = each(mul, kt, g_end)
    bhat = each(mul, bt, g_end)
    lane = lax.broadcasted_iota(I32, (c, LANES), 1)
    t_row = lax.broadcasted_iota(I32, (c, LANES), 0)
    m0 = lane < HEAD
    h0 = lambda x: jnp.where(m0, x, 0.0)
    h1 = lambda x: jnp.where(m0, 0.0, x)
    g0 = each(lambda kq, rr, bb, kk_: _nt(cat([h0(kq), h0(rr)]).astype(BF16), cat([bb, kk_]).astype(BF16)),
              kkt, rt, bt, kt)
    tick()
    g1 = each(lambda kq, rr, bb, kk_: _nt(cat([h1(rr), h1(kq)]).astype(BF16), cat([kk_, bb]).astype(BF16)),
              kkt, rt, bt, kt)
    top, left = row < c, col < c
    m_ab0, m_ab1 = top & left & (col < row), ~top & ~left & (col < row)
    m_ak0, m_ak1 = top & ~left & (col - c < row), ~top & left & (col < row - c)
    a_b = each(lambda x0, x1: jnp.where(m_ab0, x0, 0.0) + jnp.where(m_ab1, x1, 0.0), g0, g1)
    a_k = each(lambda x0, x1: jnp.where(m_ak0, x0, 0.0) + jnp.where(m_ak1, x1, 0.0), g0, g1)
    eye = jnp.where(row == col, 1.0, 0.0)
    x = each(lambda a: eye - a, a_b)
    pw = each(lambda a: _bdot(a, a), a_b)
    tick()
    for it in range(5):
        x = each(lambda xx, pp: xx + _bdot(xx, pp), x, pw)
        tick()
        if it < 4:
            pw = each(lambda pp: _bdot(pp, pp), pw)
    v0, v1 = each(h0, v), each(h1, v)
    akv = each(lambda a, va, vb: _bdot(a, cat([vb, va])), a_k, v0, v1)
    tick()
    wu = each(lambda xx, kq, av: _bdot(xx, cat([cat([h0(kq), h1(kq)]), av], axis=1)), x, kkt, akv)
    incl = jnp.where(m0, lane, lane - HEAD) <= t_row
    zeros = jnp.zeros((c, LANES), F32)
    rhs0 = each(lambda w_, va: cat([-w_[0:c], cat([zeros, va], axis=1)]), wu, v0)
    rhs1 = each(lambda w_, vb: cat([cat([zeros, vb], axis=1), -w_[c:2 * c]]), wu, v1)
    o0 = each(lambda g, rh: _bdot(jnp.where(incl, g[c:2 * c], 0.0), rh), g0, rhs0)
    o1 = each(lambda g, rh: _bdot(jnp.where(incl, g[0:c], 0.0), rh), g1, rhs1)
    rq = each(lambda rr, a0, a1: rr + a0[:, :LANES] + a1[:, :LANES], rt, o0, o1)
    yin = each(lambda a0, a1: a0[:, LANES:] + a1[:, LANES:], o0, o1)
    pd = each(lambda bh, kh, ra, rb: _tn(cat([h0(bh), h0(kh), h1(kh), h1(bh)]).astype(BF16),
                                         cat([ra, rb]).astype(BF16)), bhat, khat, rhs0, rhs1)
    phi = each(lambda p_, ge: p_[:, :LANES] + jnp.where(row == col, jnp.broadcast_to(ge, (2 * c, LANES)), 0.0),
               pd, g_end)
    return [(a, b_, c_, p_[:, LANES:]) for a, b_, c_, p_ in zip(rq, yin, phi, pd)]


def _rwkv_scan_kernel(r_ref, lw_ref, k_ref, v_ref, kk_ref, b_ref, y_ref, h_ref, lhs_ref, yin_ref, dh_ref):
    s = pl.program_id(2)
    cur = s % 2
    prev = 1 - cur
    rows = [slice(ci * RW_C, (ci + 1) * RW_C) for ci in range(RW_CH)]

    @pl.when(s == 0)
    def _():
        lhs_ref[prev] = jnp.zeros(lhs_ref.shape[1:], lhs_ref.dtype)
        yin_ref[prev] = jnp.zeros(yin_ref.shape[1:], F32)
        dh_ref[prev] = jnp.zeros(dh_ref.shape[1:], F32)
        h_ref[...] = jnp.zeros(h_ref.shape, F32)

    state = [jnp.where(s <= 1, 0.0, h_ref[...])]
    links = iter(range(RW_CH))

    def recurrence_step():
        ci = next(links, None)
        if ci is None:
            return
        both = _dot(lhs_ref[prev, ci], state[0].astype(BF16))
        y_ref[0, rows[ci], :] = both[0:RW_C] + yin_ref[prev, ci]
        state[0] = both[RW_C:] + dh_ref[prev, ci]

    ld = lambda ref: [ref[0, sl, :] for sl in rows]
    terms = _rwkv_chunk_terms(ld(r_ref), ld(lw_ref), ld(k_ref), ld(v_ref), ld(kk_ref), ld(b_ref), recurrence_step)
    for _ in range(RW_CH):
        recurrence_step()
    h_ref[...] = state[0]
    for ci, (rq, yin, phi, dh) in enumerate(terms):
        lhs_ref[cur, ci] = jnp.concatenate([rq, phi]).astype(BF16)
        yin_ref[cur, ci] = yin
        dh_ref[cur, ci] = dh


def rwkv_scan(r, lw, k2, v, kk, b, bsz, seq):
    d = r.shape[-1]
    rows = RW_C * RW_CH
    nblk = seq // rows
    in_spec = pl.BlockSpec((1, rows, LANES), lambda bb, p, s: (bb, jnp.minimum(s, nblk - 1), p))
    out_spec = pl.BlockSpec((1, rows, LANES), lambda bb, p, s: (bb, jnp.maximum(s - 1, 0), p))
    shp = lambda a: a.reshape(bsz, seq, d)
    return pl.pallas_call(
        _rwkv_scan_kernel,
        out_shape=jax.ShapeDtypeStruct((bsz, seq, d), F32),
        grid=(bsz, d // LANES, nblk + 1),
        in_specs=[in_spec] * 6,
        out_specs=out_spec,
        scratch_shapes=[pltpu.VMEM((LANES, LANES), F32),
                        pltpu.VMEM((2, RW_CH, RW_C + LANES, LANES), BF16),
                        pltpu.VMEM((2, RW_CH, RW_C, LANES), F32),
                        pltpu.VMEM((2, RW_CH, LANES, LANES), F32)],
        compiler_params=_cp("parallel", "parallel", "arbitrary"),
    )(shp(r), shp(lw), shp(k2), shp(v), shp(kk), shp(b)).reshape(bsz * seq, d)


def _rwkv_post_kernel(y_ref, r_ref, k_ref, v_ref, g_ref, lng_ref, lnb_ref, rk_ref, w_ref, x_ref, gn_ref,
                      xo_ref, ho_ref, o_scr):
    ones = _seg64_ones()
    for p in range(y_ref.shape[1] // SEG_W):
        sl = slice(p * SEG_W, (p + 1) * SEG_W)
        y = y_ref[:, sl]
        dv = y - _seg64_sum(y, ones) * (1.0 / HEAD)
        var = _seg64_sum(dv * dv, ones) * (1.0 / HEAD)
        yn = dv * lax.rsqrt(var + GN_EPS) * lng_ref[:, sl] + lnb_ref[:, sl]
        bonus = _seg64_sum(r_ref[:, sl] * k_ref[:, sl] * rk_ref[:, sl], ones) * v_ref[:, sl]
        o_scr[:, sl] = ((yn + bonus) * g_ref[:, sl]).astype(o_scr.dtype)
    xn = x_ref[...] + _dot(o_scr[...], w_ref[...])
    xo_ref[...] = xn
    ho_ref[...] = _rms(xn, gn_ref[...]).astype(ho_ref.dtype)


def rwkv_post(y, r, k2, v, g, ln_g, ln_b, r_k, w_out, x2, gn, tm=256):
    t, d = x2.shape
    row = lambda i: (i, 0)
    fixed = lambda i: (0, 0)
    tile = pl.BlockSpec((tm, d), row)
    vecs = pl.BlockSpec((1, d), fixed)
    return pl.pallas_call(
        _rwkv_post_kernel,
        out_shape=(jax.ShapeDtypeStruct((t, d), F32), jax.ShapeDtypeStruct((t, d), BF16)),
        grid=(t // tm,),
        in_specs=[tile] * 5 + [vecs] * 3 + [pl.BlockSpec((d, d), fixed), tile, vecs],
        out_specs=(tile, tile),
        scratch_shapes=[pltpu.VMEM((tm, d), BF16)],
        compiler_params=_cp("parallel"),
    )(y, r, k2, v, g, ln_g.reshape(1, d), ln_b.reshape(1, d), r_k.reshape(1, d), w_out.astype(BF16), x2,
      gn.reshape(1, d))


S5_L = 16


def _split(a):
    hi = a.astype(BF16)
    return hi, (a - hi.astype(F32)).astype(BF16)


def _s5_carry_kernel(er_ref, ei_ref, lr_ref, li_ref, pr_ref, pi_ref):
    lr = lr_ref[...]
    li = li_ref[...]

    def body(n, st):
        sr, si = st
        pr_ref[n] = sr
        pi_ref[n] = si
        return (lr * sr - li * si + er_ref[n], lr * si + li * sr + ei_ref[n])
    zero = jnp.zeros(lr.shape, F32)
    lax.fori_loop(0, er_ref.shape[0], body, (zero, zero))


S5_TG = 8
S5_NB = 256


def _dot2(a, b):
    a_hi, a_lo = _split(a)
    return _dot(a_hi, b) + _dot(a_lo, b)


def _s5_rows(x_ref):
    return jnp.concatenate([x_ref[:, t, :] for t in range(x_ref.shape[1])], axis=1)


def _s5t_local_kernel(x_ref, gr_ref, gi_ref, xr_ref, xi_ref):
    u = _s5_rows(x_ref)
    xr_ref[...] = _dot2(u, gr_ref[0])
    xi_ref[...] = _dot2(u, gi_ref[0])


def _s5t_out_kernel(x_ref, pr_ref, pi_ref, k_ref, er_ref, ei_ref, y_ref):
    y = _dot2(_s5_rows(x_ref), k_ref[0]) + _dot2(pr_ref[...], er_ref[0]) + _dot2(pi_ref[...], ei_ref[0])
    for t in range(y_ref.shape[1]):
        y_ref[:, t, :] = y[:, t * LANES:(t + 1) * LANES]


def s5_ssm(h3, a_re, a_im, log_step, b_re, b_im, c_re, c_im):
    bsz, seq, d = h3.shape
    ng, ns = a_re.shape
    gc = d // ng
    L, tg = S5_L, S5_TG
    nc = seq // L
    n = bsz * nc
    nj = ng // tg
    assert tg * gc == LANES and 2 * ns == LANES and n % S5_NB == 0 and L % 2 == 0
    step = jnp.exp(log_step.astype(F32))[:, None]
    lam = lax.complex(a_re.astype(F32), a_im.astype(F32))
    lam_bar = jnp.exp(lam * step)
    b_bar = ((lam_bar - 1.0) / lam)[..., None] * lax.complex(b_re.astype(F32), b_im.astype(F32))
    cc = lax.complex(c_re.astype(F32), c_im.astype(F32))
    pw = jnp.exp((lam * step)[:, None, :] * jnp.arange(L + 1, dtype=F32)[None, :, None])
    lag = jnp.arange(L)[None, :] - jnp.arange(L)[:, None]
    kfull = jnp.einsum('gcp,gstp,gpe->gsetc', cc, pw[:, jnp.clip(lag, 0, L)], b_bar)
    kmat = jnp.where((lag >= 0)[None, :, None, :, None], jnp.real(kfull), 0.0)
    gfull = jnp.einsum('gsp,gpe->gsep', pw[:, L - 1 - jnp.arange(L)], b_bar)
    efull = jnp.einsum('gcp,gtp->gptc', cc, pw[:, 1:])
    lam_l = pw[:, L]
    eye = jnp.eye(tg, dtype=F32)
    w = L * LANES
    k2 = jnp.einsum('jgsetc,gh->jsgethc', kmat.reshape(nj, tg, L, gc, L, gc), eye).reshape(nj, w, w).astype(BF16)
    g5 = gfull.reshape(nj, tg, L, gc, ns)
    g2r = jnp.einsum('jgsep,gh->jsgehp', jnp.real(g5), eye).reshape(nj, w, tg * ns).astype(BF16)
    g2i = jnp.einsum('jgsep,gh->jsgehp', jnp.imag(g5), eye).reshape(nj, w, tg * ns).astype(BF16)
    e5 = efull.reshape(nj, tg, ns, L, gc)
    e2r = jnp.einsum('jgptc,gh->jgpthc', jnp.real(e5), eye).reshape(nj, tg * ns, w).astype(BF16)
    e2i = jnp.einsum('jgptc,gh->jgpthc', -jnp.imag(e5), eye).reshape(nj, tg * ns, w).astype(BF16)
    nq = ng // 2
    lr = jnp.real(lam_l).reshape(nq, LANES)
    li = jnp.imag(lam_l).reshape(nq, LANES)

    x4 = h3.reshape(n, L, d)
    nb = S5_NB
    sw = tg * ns
    x_spec = pl.BlockSpec((nb, L, LANES), lambda j, i: (i, 0, j))
    xr, xi = pl.pallas_call(
        _s5t_local_kernel,
        out_shape=(jax.ShapeDtypeStruct((n, nj * sw), F32),) * 2,
        grid=(nj, n // nb),
        in_specs=[x_spec, pl.BlockSpec((1, w, sw), lambda j, i: (j, 0, 0)),
                  pl.BlockSpec((1, w, sw), lambda j, i: (j, 0, 0))],
        out_specs=(pl.BlockSpec((nb, sw), lambda j, i: (i, j)),) * 2,
        compiler_params=_cp("parallel", "parallel"),
    )(x4, g2r, g2i)
    st_spec = pl.BlockSpec((nc, 8, LANES), lambda b, q: (b, q, 0))
    lam_spec = pl.BlockSpec((8, LANES), lambda b, q: (q, 0))
    pr, pi = pl.pallas_call(
        _s5_carry_kernel,
        out_shape=(jax.ShapeDtypeStruct((n, nq, LANES), F32),) * 2,
        grid=(bsz, nq // 8),
        in_specs=[st_spec, st_spec, lam_spec, lam_spec],
        out_specs=(st_spec, st_spec),
        compiler_params=_cp("parallel", "parallel"),
    )(xr.reshape(n, nq, LANES), xi.reshape(n, nq, LANES), lr, li)
    hw = w // 2
    p_spec = pl.BlockSpec((nb, sw), lambda j, hh, i: (i, j))
    e_spec = pl.BlockSpec((1, sw, hw), lambda j, hh, i: (j, 0, hh))
    y = pl.pallas_call(
        _s5t_out_kernel,
        out_shape=jax.ShapeDtypeStruct((n, L, d), F32),
        grid=(nj, 2, n // nb),
        in_specs=[pl.BlockSpec((nb, L, LANES), lambda j, hh, i: (i, 0, j)), p_spec, p_spec,
                  pl.BlockSpec((1, w, hw), lambda j, hh, i: (j, 0, hh)), e_spec, e_spec],
        out_specs=pl.BlockSpec((nb, L // 2, LANES), lambda j, hh, i: (i, hh, j)),
        compiler_params=_cp("parallel", "parallel", "parallel"),
    )(x4, pr.reshape(n, nq * LANES), pi.reshape(n, nq * LANES), k2, e2r, e2i)
    return y.reshape(bsz * seq, d)


def _s5_glu_kernel(ys_ref, h_ref, d_ref, w_ref, b_ref, x_ref, gn_ref, xo_ref, ho_ref):
    y = ys_ref[...] + d_ref[...] * h_ref[...]
    gelu = 0.5 * y * (1.0 + jnp.tanh(math.sqrt(2.0 / math.pi) * (y + 0.044715 * (y * y * y))))
    z = _dot(gelu.astype(BF16), w_ref[...]) + b_ref[...]
    dm = x_ref.shape[1]
    xn = x_ref[...] + z[:, :dm] * jax.nn.sigmoid(z[:, dm:])
    xo_ref[...] = xn
    ho_ref[...] = _rms(xn, gn_ref[...]).astype(ho_ref.dtype)


def s5_glu(ys, h, d_skip, w_glu, b_glu, x2, gn, tm=256):
    t, d = x2.shape
    row = lambda i: (i, 0)
    fixed = lambda i: (0, 0)
    tile = pl.BlockSpec((tm, d), row)
    return pl.pallas_call(
        _s5_glu_kernel,
        out_shape=(jax.ShapeDtypeStruct((t, d), F32), jax.ShapeDtypeStruct((t, d), BF16)),
        grid=(t // tm,),
        in_specs=[tile, tile, pl.BlockSpec((1, d), fixed), pl.BlockSpec((d, 2 * d), fixed),
                  pl.BlockSpec((1, 2 * d), fixed), tile, pl.BlockSpec((1, d), fixed)],
        out_specs=(tile, tile),
        compiler_params=_cp("parallel"),
    )(ys, h, d_skip.reshape(1, d), w_glu.astype(BF16), b_glu.reshape(1, 2 * d), x2, gn.reshape(1, d))


def kernel(x, norm_mix, norm_ffn, dsa_w_in, dsa_q_norm, dsa_k_norm, dsa_kidx_norm, dsa_w_out, rwkv_mu, rwkv_w_rkv, rwkv_w0, rwkv_w1, rwkv_w2, rwkv_a0, rwkv_a1, rwkv_a2, rwkv_g1, rwkv_g2, rwkv_k_k, rwkv_k_a, rwkv_r_k, rwkv_ln_g, rwkv_ln_b, rwkv_w_out, s5_a_re, s5_a_im, s5_log_step, s5_b_re, s5_b_im, s5_c_re, s5_c_im, s5_d, s5_w_glu, s5_b_glu, ffn_w_up, ffn_conv_w, ffn_conv_b, ffn_w_down):
    bsz, seq, d = x.shape
    depth = norm_mix.shape[0]
    x2 = x.reshape(bsz * seq, d)
    for i in range(depth):
        kind, j = i % 3, i // 3
        if kind == 0:
            h = rmsnorm(x2, norm_mix[i], BF16)
            o = dsa_mixer(h, bsz, seq, dsa_w_in[j], dsa_q_norm[j], dsa_k_norm[j], dsa_kidx_norm[j])
            x2, h = mm_res_norm(o, dsa_w_out[j].astype(BF16), x2, norm_ffn[i], BF16)
        elif kind == 1:
            r, lw, k2, v, kk, b, g = rwkv_proj(x2, seq, norm_mix[i], rwkv_mu[j], rwkv_w_rkv[j], rwkv_w0[j],
                                               rwkv_w1[j], rwkv_w2[j], rwkv_a0[j], rwkv_a1[j], rwkv_a2[j],
                                               rwkv_g1[j], rwkv_g2[j], rwkv_k_k[j], rwkv_k_a[j])
            y = rwkv_scan(r, lw, k2, v, kk, b, bsz, seq)
            x2, h = rwkv_post(y, r, k2, v, g, rwkv_ln_g[j], rwkv_ln_b[j], rwkv_r_k[j].reshape(d), rwkv_w_out[j],
                              x2, norm_ffn[i])
        else:
            hf = rmsnorm(x2, norm_mix[i], F32)
            ys = s5_ssm(hf.reshape(bsz, seq, d), s5_a_re[j], s5_a_im[j], s5_log_step[j], s5_b_re[j], s5_b_im[j],
                        s5_c_re[j], s5_c_im[j])
            x2, h = s5_glu(ys, hf, s5_d[j], s5_w_glu[j], s5_b_glu[j], x2, norm_ffn[i])
        act = ffn_up(h, ffn_w_up[i].astype(BF16), ffn_conv_w[i], ffn_conv_b[i], seq)
        x2, _ = mm_res_norm(act, ffn_w_down[i].astype(BF16), x2, None, None)
    return x2.reshape(bsz, seq, d)
```

```python
import functools
import math

import jax
import jax.numpy as jnp
from jax import lax
from jax.experimental import pallas as pl
from jax.experimental.pallas import tpu as pltpu

F32 = jnp.float32
BF16 = jnp.bfloat16
I32 = jnp.int32

EPS = 1e-6
NEG_INF = -1e30
LANES = 128
HEAD = 64
CHUNK = 64
TOPK_MAX = 256
GN_EPS = 64e-5
INT_MIN = -(2 ** 31)

VMEM_LIMIT = 56 * 1024 * 1024


def _cp(*sem):
    return pltpu.CompilerParams(dimension_semantics=sem, vmem_limit_bytes=VMEM_LIMIT)


def _nt(a, b):
    return lax.dot_general(a, b, (((1,), (1,)), ((), ())), preferred_element_type=F32)


def _tn(a, b):
    return lax.dot_general(a, b, (((0,), (0,)), ((), ())), preferred_element_type=F32)


def _dot(a, b):
    return jnp.dot(a, b, preferred_element_type=F32)


def _rms(x, g):
    ms = jnp.mean(x * x, axis=-1, keepdims=True)
    return x * lax.rsqrt(ms + EPS) * g


SEG_W = 2 * LANES


def _seg64_ones():
    r = lax.broadcasted_iota(I32, (SEG_W, SEG_W), 0) // HEAD
    c = lax.broadcasted_iota(I32, (SEG_W, SEG_W), 1) // HEAD
    return jnp.where(r == c, 1.0, 0.0).astype(BF16)


def _seg64_sum(x, ones):
    hi = x.astype(BF16)
    lo = (x - hi.astype(F32)).astype(BF16)
    return _dot(hi, ones) + _dot(lo, ones)


def _norm_kernel(x_ref, g_ref, o_ref):
    o_ref[...] = _rms(x_ref[...], g_ref[...]).astype(o_ref.dtype)


def rmsnorm(x2, g, out_dtype, tm=1024):
    t, d = x2.shape
    return pl.pallas_call(
        _norm_kernel,
        out_shape=jax.ShapeDtypeStruct((t, d), out_dtype),
        grid=(t // tm,),
        in_specs=[pl.BlockSpec((tm, d), lambda i: (i, 0)),
                  pl.BlockSpec((1, d), lambda i: (0, 0))],
        out_specs=pl.BlockSpec((tm, d), lambda i: (i, 0)),
        compiler_params=_cp("parallel"),
    )(x2, g.reshape(1, d))


def _mm_res_norm_kernel(a_ref, w_ref, x_ref, g_ref, xo_ref, ho_ref):
    xn = x_ref[...] + _dot(a_ref[...], w_ref[...])
    xo_ref[...] = xn
    ho_ref[...] = _rms(xn, g_ref[...]).astype(ho_ref.dtype)


def _mm_res_kernel(a_ref, w_ref, x_ref, xo_ref):
    xo_ref[...] = x_ref[...] + _dot(a_ref[...], w_ref[...])


def mm_res_norm(a, w, x2, g, h_dtype, tm=512):
    t, k = a.shape
    d = w.shape[1]
    row = lambda i: (i, 0)
    fixed = lambda i: (0, 0)
    in_specs = [pl.BlockSpec((tm, k), row), pl.BlockSpec((k, d), fixed), pl.BlockSpec((tm, d), row)]
    if g is None:
        return pl.pallas_call(
            _mm_res_kernel,
            out_shape=jax.ShapeDtypeStruct((t, d), F32),
            grid=(t // tm,), in_specs=in_specs, out_specs=pl.BlockSpec((tm, d), row),
            compiler_params=_cp("parallel"),
        )(a, w, x2), None
    return pl.pallas_call(
        _mm_res_norm_kernel,
        out_shape=(jax.ShapeDtypeStruct((t, d), F32), jax.ShapeDtypeStruct((t, d), h_dtype)),
        grid=(t // tm,),
        in_specs=in_specs + [pl.BlockSpec((1, d), fixed)],
        out_specs=(pl.BlockSpec((tm, d), row), pl.BlockSpec((tm, d), row)),
        compiler_params=_cp("parallel"),
    )(a, w, x2, g.reshape(1, d))


PREV_ROWS = 16


FFN_TN = 256


def _ffn_up_kernel(h_ref, hp_ref, w_ref, cw_ref, cb_ref, o_ref, *, tiles_per_seq):
    i = pl.program_id(0)
    h = h_ref[...]
    hp = hp_ref[...]
    f = o_ref.shape[1]
    seq_start = (i % tiles_per_seq) == 0
    head = PREV_ROWS
    rows = lax.broadcasted_iota(I32, (head, FFN_TN), 0)

    def project(c):
        out = []
        for base in (0, f):
            cols = slice(base + c * FFN_TN, base + (c + 1) * FFN_TN)
            w = w_ref[:, cols]
            out.append((_dot(h, w), jnp.where(seq_start, 0.0, _dot(hp, w)), cols))
        return out

    def conv(u, up, cols):
        cw = cw_ref[:, cols]
        cb = cb_ref[:, cols]
        mix = lambda u2, u1, u0: cw[0:1, :] * u2 + cw[1:2, :] * u1 + cw[2:3, :] * u0 + cb
        full = mix(pltpu.roll(u, 2, 0), pltpu.roll(u, 1, 0), u)
        uh = u[0:head]
        last = up[PREV_ROWS - 1:PREV_ROWS, :]
        u1 = jnp.where(rows == 0, last, pltpu.roll(uh, 1, 0))
        u2 = jnp.where(rows == 0, up[PREV_ROWS - 2:PREV_ROWS - 1, :],
                       jnp.where(rows == 1, last, pltpu.roll(uh, 2, 0)))
        return full, mix(u2, u1, uh)

    n_chunks = f // FFN_TN
    nxt = project(0)
    for c in range(n_chunks):
        (ug, upg, cg), (uv, upv, cv) = nxt
        if c + 1 < n_chunks:
            nxt = project(c + 1)
        gate, gate_h = conv(ug, upg, cg)
        val, val_h = conv(uv, upv, cv)
        o_ref[:, cg] = (gate * jax.nn.sigmoid(gate) * val).astype(o_ref.dtype)
        o_ref[0:head, cg] = (gate_h * jax.nn.sigmoid(gate_h) * val_h).astype(o_ref.dtype)


def ffn_up(h, w_up, conv_w, conv_b, seq, tm=512):
    t, d = h.shape
    f = w_up.shape[1] // 2
    assert f % FFN_TN == 0 and seq % tm == 0
    cw = jnp.zeros((8, 2 * f), F32).at[:conv_w.shape[0]].set(conv_w)
    cb = conv_b.reshape(1, 2 * f)
    fixed = lambda i: (0, 0)
    return pl.pallas_call(
        functools.partial(_ffn_up_kernel, tiles_per_seq=seq // tm),
        out_shape=jax.ShapeDtypeStruct((t, f), BF16),
        grid=(t // tm,),
        in_specs=[pl.BlockSpec((tm, d), lambda i: (i, 0)),
                  pl.BlockSpec((PREV_ROWS, d), lambda i: (jnp.maximum(i * (tm // PREV_ROWS) - 1, 0), 0)),
                  pl.BlockSpec((d, 2 * f), fixed),
                  pl.BlockSpec((8, 2 * f), fixed),
                  pl.BlockSpec((1, 2 * f), fixed)],
        out_specs=pl.BlockSpec((tm, f), lambda i: (i, 0)),
        compiler_params=_cp("parallel"),
    )(h, h, w_up, cw, cb)


PAIRS = 8
IDX_PAIRS = 4
TQ = 128
TK = 512


def _dsa_proj_kernel(h_ref, wq_ref, wk_ref, wvt_ref, wi_ref, gq_ref, gk_ref, gki_ref,
                     q_ref, k_ref, vt_ref, qi_ref, ki_ref, wi_out_ref):
    h = h_ref[...]
    tm = h.shape[0]

    ones = _seg64_ones()

    def head_norm(y, g_ref, o_ref):
        for p2 in range(PAIRS // 2):
            sl = slice(p2 * SEG_W, (p2 + 1) * SEG_W)
            yp = y[:, sl]
            ms = _seg64_sum(yp * yp, ones) * (1.0 / HEAD)
            yn = (yp * lax.rsqrt(ms + EPS) * g_ref[:, sl]).astype(o_ref.dtype)
            o_ref[0, 2 * p2] = yn[:, :LANES]
            o_ref[0, 2 * p2 + 1] = yn[:, LANES:]

    head_norm(_dot(h, wq_ref[...]), gq_ref, q_ref)
    head_norm(_dot(h, wk_ref[...]), gk_ref, k_ref)
    vt = _nt(wvt_ref[...], h)
    vt_ref[0, :, :LANES, :] = vt.reshape(PAIRS, LANES, tm).astype(vt_ref.dtype)
    vt_ref[0, :, LANES:, :] = jnp.ones((PAIRS, vt_ref.shape[2] - LANES, tm), vt_ref.dtype)
    idx = _dot(h, wi_ref[...])
    for p in range(IDX_PAIRS):
        qi_ref[0, p] = (idx[:, p * LANES:(p + 1) * LANES] * (HEAD ** -0.5)).astype(qi_ref.dtype)
    kw = idx[:, IDX_PAIRS * LANES:]
    ms = _seg64_sum(kw * kw, ones)[:, :LANES] * (1.0 / HEAD)
    ki_ref[0] = (kw[:, :LANES] * lax.rsqrt(ms + EPS) * gki_ref[...]).astype(ki_ref.dtype)
    wi_out_ref[0] = idx[:, (IDX_PAIRS + 1) * LANES:] * (2 * IDX_PAIRS) ** -0.5


def dsa_proj(h, bsz, seq, w_in, q_gain, k_gain, kidx_gain, tm=256):
    t, d = h.shape
    n_idx = 2 * IDX_PAIRS
    wq = w_in[:, :d].astype(BF16)
    wk = w_in[:, d:2 * d].astype(BF16)
    wvt = w_in[:, 2 * d:3 * d].T.astype(BF16)
    o = 3 * d
    w_qi = w_in[:, o:o + n_idx * HEAD]
    w_ki = w_in[:, o + n_idx * HEAD:o + n_idx * HEAD + HEAD]
    w_wi = w_in[:, o + n_idx * HEAD + HEAD:]
    w_wi = jnp.pad(w_wi, ((0, 0), (0, LANES - n_idx)))
    wi = jnp.concatenate([w_qi, w_ki, w_ki, w_wi], axis=1).astype(BF16)
    gq = (jnp.tile(q_gain, d // HEAD) * (HEAD ** -0.5 * math.log2(math.e))).reshape(1, d)
    gk = jnp.tile(k_gain, d // HEAD).reshape(1, d)
    gki = jnp.tile(kidx_gain, 2).reshape(1, LANES)
    nb = seq // tm
    fixed = lambda b, i: (0, 0)
    return pl.pallas_call(
        _dsa_proj_kernel,
        out_shape=(jax.ShapeDtypeStruct((bsz, PAIRS, seq, LANES), BF16),
                   jax.ShapeDtypeStruct((bsz, PAIRS, seq, LANES), BF16),
                   jax.ShapeDtypeStruct((bsz, PAIRS, VT_ROWS, seq), BF16),
                   jax.ShapeDtypeStruct((bsz, IDX_PAIRS, seq, LANES), BF16),
                   jax.ShapeDtypeStruct((bsz, seq, LANES), BF16),
                   jax.ShapeDtypeStruct((bsz, seq, LANES), F32)),
        grid=(bsz, nb),
        in_specs=[pl.BlockSpec((tm, d), lambda b, i: (b * nb + i, 0)),
                  pl.BlockSpec((d, d), fixed), pl.BlockSpec((d, d), fixed), pl.BlockSpec((d, d), fixed),
                  pl.BlockSpec((d, wi.shape[1]), fixed),
                  pl.BlockSpec((1, d), fixed), pl.BlockSpec((1, d), fixed), pl.BlockSpec((1, LANES), fixed)],
        out_specs=(pl.BlockSpec((1, PAIRS, tm, LANES), lambda b, i: (b, 0, i, 0)),
                   pl.BlockSpec((1, PAIRS, tm, LANES), lambda b, i: (b, 0, i, 0)),
                   pl.BlockSpec((1, PAIRS, VT_ROWS, tm), lambda b, i: (b, 0, 0, i)),
                   pl.BlockSpec((1, IDX_PAIRS, tm, LANES), lambda b, i: (b, 0, i, 0)),
                   pl.BlockSpec((1, tm, LANES), lambda b, i: (b, i, 0)),
                   pl.BlockSpec((1, tm, LANES), lambda b, i: (b, i, 0))),
        compiler_params=_cp("parallel", "parallel"),
    )(h, wq, wk, wvt, wi, gq, gk, gki)


def _pair_split(x):
    lane = lax.broadcasted_iota(I32, x.shape, 1)
    zero = jnp.zeros_like(x)
    return jnp.concatenate([jnp.where(lane < HEAD, x, zero), jnp.where(lane >= HEAD, x, zero)], axis=0)


EXP_MASK = 0x7F800000


def _dsa_select_kernel(qi_ref, ki_ref, wi_ref, mask_ref, keys_ref, *, seq, topk):
    j = pl.program_id(1)
    lane = lax.broadcasted_iota(I32, (1, TQ), 1)
    limq = j * TQ + (lane // CHUNK + 1) * CHUNK
    nkt = (j * TQ + TQ + TK - 1) // TK
    wit = wi_ref[0].T
    w_pairs = [_pair_split(qi_ref[0, p]) for p in range(IDX_PAIRS)]
    kiota = lax.broadcasted_iota(I32, (TK, TQ), 0)

    def score_body(kt, carry):
        off = pl.multiple_of(kt * TK, TK)
        kit = ki_ref[0, pl.ds(off, TK), :]
        lgs = [_nt(kit, w_pairs[p]) for p in range(IDX_PAIRS)]
        s = jnp.zeros((TK, TQ), F32)
        for p in range(IDX_PAIRS):
            s = s + jnp.maximum(lgs[p][:, :TQ], 0.0) * wit[2 * p:2 * p + 1, :]
            s = s + jnp.maximum(lgs[p][:, TQ:], 0.0) * wit[2 * p + 1:2 * p + 2, :]
        s = jnp.where(off + kiota < limq, s, NEG_INF)
        bits = pltpu.bitcast(s, I32)
        bits = jnp.where((bits & EXP_MASK) == 0, 0, bits)
        keys_ref[pl.ds(off, TK), :] = jnp.where(bits < 0, bits ^ 0x7FFFFFFF, bits)
        return carry

    lax.fori_loop(0, nkt, score_body, 0)

    def count(pred):
        def body(i, acc):
            off = pl.multiple_of(i * TK, TK)
            hit = jnp.where(pred(keys_ref[pl.ds(off, TK), :], off), 1, 0)
            return acc + hit.reshape(TK // 8, 8, TQ).sum(axis=0)
        acc = lax.fori_loop(0, nkt, body, jnp.zeros((8, TQ), I32))
        return acc.sum(axis=0, keepdims=True)

    def search(_):
        def bit_body(t, tu):
            cand_u = tu | jnp.left_shift(jnp.int32(1), 31 - t)
            cand_s = cand_u ^ INT_MIN
            c = count(lambda kb, off: kb >= cand_s)
            return jnp.where(c >= topk, cand_u, tu)
        tu = lax.fori_loop(0, 32, bit_body, jnp.zeros((1, TQ), I32))
        return tu ^ INT_MIN

    ts = lax.cond(j * TQ + TQ > topk, search, lambda _: jnp.full((1, TQ), INT_MIN, I32), 0)

    n_ge = count(lambda kb, off: kb >= ts)

    @pl.when(jnp.max(n_ge) > topk)
    def _break_ties():
        r = topk - count(lambda kb, off: kb > ts)

        def bit_body(t, p):
            cand = p | jnp.left_shift(jnp.int32(1), (seq.bit_length() - 1) - t)
            c = count(lambda kb, off: (kb == ts) & (off + kiota < cand))
            return jnp.where(c < r, cand, p)
        p_last = lax.fori_loop(0, seq.bit_length(), bit_body, jnp.zeros((1, TQ), I32))

        def demote(kt, carry):
            off = pl.multiple_of(kt * TK, TK)
            kb = keys_ref[pl.ds(off, TK), :]
            keys_ref[pl.ds(off, TK), :] = jnp.where((kb == ts) & (off + kiota > p_last), ts - 1, kb)
            return carry
        lax.fori_loop(0, nkt, demote, 0)

    def out_body(kt, carry):
        off = pl.multiple_of(kt * TK, TK)

        @pl.when(kt < nkt)
        def _():
            kb = keys_ref[pl.ds(off, TK), :]
            sel = (kb >= ts) & (off + kiota < limq)
            mask_ref[0, 0, pl.ds(off, TK), :] = jnp.where(sel, 1, 0).astype(mask_ref.dtype)

        @pl.when(kt >= nkt)
        def _():
            mask_ref[0, 0, pl.ds(off, TK), :] = jnp.zeros((TK, TQ), mask_ref.dtype)
        return carry

    lax.fori_loop(0, seq // TK, out_body, 0)


def dsa_select(qi, ki, wi, topk):
    bsz, _, seq, _ = qi.shape
    nq = seq // TQ
    return pl.pallas_call(
        functools.partial(_dsa_select_kernel, seq=seq, topk=topk),
        out_shape=jax.ShapeDtypeStruct((bsz, nq, seq, TQ), jnp.int8),
        grid=(bsz, nq),
        in_specs=[pl.BlockSpec((1, IDX_PAIRS, TQ, LANES), lambda b, j: (b, 0, j, 0)),
                  pl.BlockSpec((1, seq, LANES), lambda b, j: (b, 0, 0)),
                  pl.BlockSpec((1, TQ, LANES), lambda b, j: (b, j, 0))],
        out_specs=pl.BlockSpec((1, 1, seq, TQ), lambda b, j: (b, j, 0, 0)),
        scratch_shapes=[pltpu.VMEM((seq, TQ), I32)],
        compiler_params=_cp("parallel", "parallel"),
    )(qi, ki, wi)


VT_ROWS = LANES + 16
S_CHUNK = 64
S_AHEAD = 3
S_SLOTS = S_AHEAD + 1


def _dsa_attn_kernel(jmap_ref, ktmap_ref, q_ref, k_ref, vt_ref, mask_ref, o_ref,
                     qm_ref, m_ref, l_ref, acc_ref, bias_ref, s_ref, p_ref):
    step = pl.program_id(1)
    j = jmap_ref[step]
    kt = ktmap_ref[step]
    last = (j * TQ + TQ - 1) // TK

    @pl.when(kt == 0)
    def _init():
        for p in range(PAIRS):
            qm_ref[p] = _pair_split(q_ref[0, p].astype(F32)).T.astype(BF16)
        m_ref[...] = jnp.full(m_ref.shape, -jnp.inf, F32)
        l_ref[...] = jnp.zeros(l_ref.shape, F32)
        acc_ref[...] = jnp.zeros(acc_ref.shape, F32)

    bias = (mask_ref[0, 0].astype(F32) - 1.0) * 1e30
    bias_ref[:, :TQ] = bias
    bias_ref[:, TQ:] = bias

    def scores(p):
        m_tile = None
        for hf in range(2):
            rows = slice(hf * (TK // 2), (hf + 1) * (TK // 2))
            s = _dot(k_ref[0, p, rows, :], qm_ref[p]) + bias_ref[rows, :]
            s_ref[p % S_SLOTS, rows, :] = s
            m_half = jnp.max(s, axis=0, keepdims=True)
            m_tile = m_half if m_tile is None else jnp.maximum(m_tile, m_half)
        return m_tile

    m_tiles = [scores(p) for p in range(S_AHEAD)]
    for p in range(PAIRS):
        slot = p % S_SLOTS
        m_old = m_ref[p]
        m_new = jnp.maximum(m_old, m_tiles[p])
        alpha = jnp.exp2(m_old - m_new)
        m_ref[p] = m_new
        if p + S_AHEAD < PAIRS:
            m_tiles.append(scores(p + S_AHEAD))
        for c in range(TK // S_CHUNK):
            rows = slice(c * S_CHUNK, (c + 1) * S_CHUNK)
            p_ref[slot, rows, :] = jnp.exp2(s_ref[slot, rows, :] - m_new).astype(BF16)
        pv = _dot(vt_ref[0, p], p_ref[slot])
        acc_ref[p] = acc_ref[p] * alpha + pv[:LANES]
        l_ref[p] = l_ref[p] * alpha + pv[LANES:LANES + 1]

    @pl.when(kt == last)
    def _finish():
        for p in range(PAIRS):
            a = acc_ref[p] / l_ref[p]
            ot = jnp.concatenate([a[0:HEAD, 0:TQ], a[HEAD:2 * HEAD, TQ:2 * TQ]], axis=0)
            o_ref[0, :, p * LANES:(p + 1) * LANES] = ot.T.astype(o_ref.dtype)


def dsa_attn(q, k, vt, mask):
    bsz, _, seq, _ = q.shape
    nq = seq // TQ
    visits = [(j, t) for j in range(nq) for t in range((j * TQ + TQ - 1) // TK + 1)]
    jmap = jnp.asarray([jt[0] for jt in visits], I32)
    ktmap = jnp.asarray([jt[1] for jt in visits], I32)
    grid_spec = pltpu.PrefetchScalarGridSpec(
        num_scalar_prefetch=2,
        grid=(bsz, len(visits)),
        in_specs=[pl.BlockSpec((1, PAIRS, TQ, LANES), lambda b, s, jm, km: (b, 0, jm[s], 0)),
                  pl.BlockSpec((1, PAIRS, TK, LANES), lambda b, s, jm, km: (b, 0, km[s], 0)),
                  pl.BlockSpec((1, PAIRS, VT_ROWS, TK), lambda b, s, jm, km: (b, 0, 0, km[s])),
                  pl.BlockSpec((1, 1, TK, TQ), lambda b, s, jm, km: (b, jm[s], km[s], 0))],
        out_specs=pl.BlockSpec((1, TQ, PAIRS * LANES), lambda b, s, jm, km: (b, jm[s], 0)),
        scratch_shapes=[pltpu.VMEM((PAIRS, LANES, 2 * TQ), BF16),
                        pltpu.VMEM((PAIRS, 1, 2 * TQ), F32),
                        pltpu.VMEM((PAIRS, 1, 2 * TQ), F32),
                        pltpu.VMEM((PAIRS, LANES, 2 * TQ), F32),
                        pltpu.VMEM((TK, 2 * TQ), F32),
                        pltpu.VMEM((S_SLOTS, TK, 2 * TQ), F32),
                        pltpu.VMEM((S_SLOTS, TK, 2 * TQ), BF16)])
    return pl.pallas_call(
        _dsa_attn_kernel,
        out_shape=jax.ShapeDtypeStruct((bsz, seq, PAIRS * LANES), BF16),
        grid_spec=grid_spec,
        compiler_params=_cp("parallel", "arbitrary"),
    )(jmap, ktmap, q, k, vt, mask)


def dsa_mixer(h, bsz, seq, w_in, q_gain, k_gain, kidx_gain):
    assert seq % TK == 0 and TK % TQ == 0
    topk = min(TOPK_MAX, seq // 4)
    q, k, vt, qi, ki, wi = dsa_proj(h, bsz, seq, w_in, q_gain, k_gain, kidx_gain)
    mask = dsa_select(qi, ki, wi, topk)
    return dsa_attn(q, k, vt, mask).reshape(bsz * seq, PAIRS * LANES)


RW_PREV = 8
RW_C = 64
RW_CH = 8


def _rwkv_proj_kernel(x_ref, xp_ref, gn_ref, mu_ref, wr_ref, wk_ref, wv_ref, w1_ref, w2_ref, a1_ref, a2_ref,
                      g1_ref, g2_ref, w0_ref, a0_ref, kk_ref, ka_ref,
                      r_out, lw_out, k_out, v_out, kk_out, b_out, g_out, *, tiles_per_seq):
    i = pl.program_id(0)
    gn = gn_ref[...]
    h = _rms(x_ref[...], gn)
    hp = _rms(xp_ref[...], gn)
    hp_last = jnp.where((i % tiles_per_seq) == 0, 0.0, hp[RW_PREV - 1:RW_PREV, :])
    rows = lax.broadcasted_iota(I32, h.shape, 0)
    dh = jnp.where(rows == 0, hp_last, pltpu.roll(h, 1, 0)) - h
    mu = mu_ref[...]
    xs = lambda n: (h + dh * mu[n:n + 1, :]).astype(BF16)
    r = _dot(xs(0), wr_ref[...])
    k = _dot(xs(1), wk_ref[...])
    v_out[...] = _dot(xs(2), wv_ref[...])
    wl = w0_ref[...] + _dot(jnp.tanh(_dot(xs(3), w1_ref[...])).astype(BF16), w2_ref[...])
    z = -wl
    w_log = -(jnp.maximum(z, 0.0) + jnp.log(1.0 + jnp.exp(-jnp.abs(z)))) - 0.5
    lw_out[...] = -jnp.exp(w_log)
    a = jax.nn.sigmoid(a0_ref[...] + _dot(_dot(xs(4), a1_ref[...]).astype(BF16), a2_ref[...]))
    g_out[...] = _dot(jax.nn.sigmoid(_dot(xs(5), g1_ref[...])).astype(BF16), g2_ref[...])
    r_out[...] = r
    k_out[...] = k * (1.0 + (a - 1.0) * ka_ref[...])
    kk = k * kk_ref[...]
    ones = _seg64_ones()
    for p in range(kk.shape[1] // SEG_W):
        sl = slice(p * SEG_W, (p + 1) * SEG_W)
        kp = kk[:, sl]
        kn = kp / jnp.maximum(jnp.sqrt(_seg64_sum(kp * kp, ones)), 1e-12)
        kk_out[:, sl] = kn
        b_out[:, sl] = kn * a[:, sl]


def rwkv_proj(x2, seq, gn, mu, w_rkv, w0, w1, w2, a0, a1, a2, g1, g2, k_k, k_a, tm=256):
    t, d = x2.shape
    pad_c = lambda w: jnp.pad(w, ((0, 0), (0, LANES - w.shape[1]))).astype(BF16)
    pad_r = lambda w: jnp.pad(w, ((0, LANES - w.shape[0]), (0, 0))).astype(BF16)
    row = lambda i: (i, 0)
    fixed = lambda i: (0, 0)
    vec = lambda a: a.reshape(1, d)
    full = lambda a: pl.BlockSpec(a.shape, fixed)
    args = [x2, x2, vec(gn), jnp.pad(mu, ((0, 2), (0, 0))),
            w_rkv[0].astype(BF16), w_rkv[1].astype(BF16), w_rkv[2].astype(BF16),
            pad_c(w1), pad_r(w2), pad_c(a1), pad_r(a2), g1.astype(BF16), g2.astype(BF16),
            vec(w0), vec(a0), vec(k_k), vec(k_a)]
    in_specs = [pl.BlockSpec((tm, d), row),
                pl.BlockSpec((RW_PREV, d), lambda i: (jnp.maximum(i * (tm // RW_PREV) - 1, 0), 0))]
    in_specs += [full(a) for a in args[2:]]
    return pl.pallas_call(
        functools.partial(_rwkv_proj_kernel, tiles_per_seq=seq // tm),
        out_shape=tuple(jax.ShapeDtypeStruct((t, d), F32) for _ in range(7)),
        grid=(t // tm,),
        in_specs=in_specs,
        out_specs=tuple(pl.BlockSpec((tm, d), row) for _ in range(7)),
        compiler_params=_cp("parallel"),
    )(*args)


def _bdot(a, b):
    return _dot(a.astype(BF16), b.astype(BF16))


def _rwkv_chunk_terms(r, lw, k2, v, kk, b, tick):
    c = RW_C
    cat = jnp.concatenate
    each = lambda f, *ls: [f(*a) for a in zip(*ls)]
    row = lax.broadcasted_iota(I32, (2 * c, 2 * c), 0)
    col = lax.broadcasted_iota(I32, (2 * c, 2 * c), 1)
    r_c = lax.broadcasted_iota(I32, (c, c), 0)
    c_c = lax.broadcasted_iota(I32, (c, c), 1)
    tri = jnp.where(c_c <= r_c, 1.0, 0.0).astype(BF16)

    def cum(lw_):
        hi = lw_.astype(BF16)
        rem = lw_ - hi.astype(F32)
        mid = rem.astype(BF16)
        lo = (rem - mid.astype(F32)).astype(BF16)
        return _dot(tri, hi) + _dot(tri, mid) + _dot(tri, lo)
    cs = each(cum, lw)
    e_pos = each(jnp.exp, cs)
    e_neg = each(lambda x: jnp.exp(-x), cs)
    mul = lambda x, y: x * y
    rt = each(mul, r, e_pos)
    kt = each(mul, k2, e_neg)
    bt = each(mul, b, e_neg)
    kkt = each(lambda x, s, l: x * jnp.exp(s - l), kk, cs, lw)
    g_end = each(lambda e: e[c - 1:c, :], e_pos)
    khat = each(mul, kt, g_end)
    bhat = each(mul, bt, g_end)
    lane = lax.broadcasted_iota(I32, (c, LANES), 1)
    t_row = lax.broadcasted_iota(I32, (c, LANES), 0)
    m0 = lane < HEAD
    h0 = lambda x: jnp.where(m0, x, 0.0)
    h1 = lambda x: jnp.where(m0, 0.0, x)
    g0 = each(lambda kq, rr, bb, kk_: _nt(cat([h0(kq), h0(rr)]).astype(BF16), cat([bb, kk_]).astype(BF16)),
              kkt, rt, bt, kt)
    tick()
    g1 = each(lambda kq, rr, bb, kk_: _nt(cat([h1(rr), h1(kq)]).astype(BF16), cat([kk_, bb]).astype(BF16)),
              kkt, rt, bt, kt)
    top, left = row < c, col < c
    m_ab0, m_ab1 = top & left & (col < row), ~top & ~left & (col < row)
    m_ak0, m_ak1 = top & ~left & (col - c < row), ~top & left & (col < row - c)
    a_b = each(lambda x0, x1: jnp.where(m_ab0, x0, 0.0) + jnp.where(m_ab1, x1, 0.0), g0, g1)
    a_k = each(lambda x0, x1: jnp.where(m_ak0, x0, 0.0) + jnp.where(m_ak1, x1, 0.0), g0, g1)
    eye = jnp.where(row == col, 1.0, 0.0)
    x = each(lambda a: eye - a, a_b)
    pw = each(lambda a: _bdot(a, a), a_b)
    tick()
    for it in range(5):
        x = each(lambda xx, pp: xx + _bdot(xx, pp), x, pw)
        tick()
        if it < 4:
            pw = each(lambda pp: _bdot(pp, pp), pw)
    v0, v1 = each(h0, v), each(h1, v)
    akv = each(lambda a, va, vb: _bdot(a, cat([vb, va])), a_k, v0, v1)
    tick()
    wu = each(lambda xx, kq, av: _bdot(xx, cat([cat([h0(kq), h1(kq)]), av], axis=1)), x, kkt, akv)
    incl = jnp.where(m0, lane, lane - HEAD) <= t_row
    zeros = jnp.zeros((c, LANES), F32)
    rhs0 = each(lambda w_, va: cat([-w_[0:c], cat([zeros, va], axis=1)]), wu, v0)
    rhs1 = each(lambda w_, vb: cat([cat([zeros, vb], axis=1), -w_[c:2 * c]]), wu, v1)
    o0 = each(lambda g, rh: _bdot(jnp.where(incl, g[c:2 * c], 0.0), rh), g0, rhs0)
    o1 = each(lambda g, rh: _bdot(jnp.where(incl, g[0:c], 0.0), rh), g1, rhs1)
    rq = each(lambda rr, a0, a1: rr + a0[:, :LANES] + a1[:, :LANES], rt, o0, o1)
    yin = each(lambda a0, a1: a0[:, LANES:] + a1[:, LANES:], o0, o1)
    pd = each(lambda bh, kh, ra, rb: _tn(cat([h0(bh), h0(kh), h1(kh), h1(bh)]).astype(BF16),
                                         cat([ra, rb]).astype(BF16)), bhat, khat, rhs0, rhs1)
    phi = each(lambda p_, ge: p_[:, :LANES] + jnp.where(row == col, jnp.broadcast_to(ge, (2 * c, LANES)), 0.0),
               pd, g_end)
    return [(a, b_, c_, p_[:, LANES:]) for a, b_, c_, p_ in zip(rq, yin, phi, pd)]


def _rwkv_scan_kernel(r_ref, lw_ref, k_ref, v_ref, kk_ref, b_ref, y_ref, h_ref, lhs_ref, yin_ref, dh_ref):
    s = pl.program_id(2)
    cur = s % 2
    prev = 1 - cur
    rows = [slice(ci * RW_C, (ci + 1) * RW_C) for ci in range(RW_CH)]

    @pl.when(s == 0)
    def _():
        lhs_ref[prev] = jnp.zeros(lhs_ref.shape[1:], lhs_ref.dtype)
        yin_ref[prev] = jnp.zeros(yin_ref.shape[1:], F32)
        dh_ref[prev] = jnp.zeros(dh_ref.shape[1:], F32)
        h_ref[...] = jnp.zeros(h_ref.shape, F32)

    state = [jnp.where(s <= 1, 0.0, h_ref[...])]
    links = iter(range(RW_CH))

    def recurrence_step():
        ci = next(links, None)
        if ci is None:
            return
        both = _dot(lhs_ref[prev, ci], state[0].astype(BF16))
        y_ref[0, rows[ci], :] = both[0:RW_C] + yin_ref[prev, ci]
        state[0] = both[RW_C:] + dh_ref[prev, ci]

    ld = lambda ref: [ref[0, sl, :] for sl in rows]
    terms = _rwkv_chunk_terms(ld(r_ref), ld(lw_ref), ld(k_ref), ld(v_ref), ld(kk_ref), ld(b_ref), recurrence_step)
    for _ in range(RW_CH):
        recurrence_step()
    h_ref[...] = state[0]
    for ci, (rq, yin, phi, dh) in enumerate(terms):
        lhs_ref[cur, ci] = jnp.concatenate([rq, phi]).astype(BF16)
        yin_ref[cur, ci] = yin
        dh_ref[cur, ci] = dh


def rwkv_scan(r, lw, k2, v, kk, b, bsz, seq):
    d = r.shape[-1]
    rows = RW_C * RW_CH
    nblk = seq // rows
    in_spec = pl.BlockSpec((1, rows, LANES), lambda bb, p, s: (bb, jnp.minimum(s, nblk - 1), p))
    out_spec = pl.BlockSpec((1, rows, LANES), lambda bb, p, s: (bb, jnp.maximum(s - 1, 0), p))
    shp = lambda a: a.reshape(bsz, seq, d)
    return pl.pallas_call(
        _rwkv_scan_kernel,
        out_shape=jax.ShapeDtypeStruct((bsz, seq, d), F32),
        grid=(bsz, d // LANES, nblk + 1),
        in_specs=[in_spec] * 6,
        out_specs=out_spec,
        scratch_shapes=[pltpu.VMEM((LANES, LANES), F32),
                        pltpu.VMEM((2, RW_CH, RW_C + LANES, LANES), BF16),
                        pltpu.VMEM((2, RW_CH, RW_C, LANES), F32),
                        pltpu.VMEM((2, RW_CH, LANES, LANES), F32)],
        compiler_params=_cp("parallel", "parallel", "arbitrary"),
    )(shp(r), shp(lw), shp(k2), shp(v), shp(kk), shp(b)).reshape(bsz * seq, d)


def _rwkv_post_kernel(y_ref, r_ref, k_ref, v_ref, g_ref, lng_ref, lnb_ref, rk_ref, w_ref, x_ref, gn_ref,
                      xo_ref, ho_ref, o_scr):
    ones = _seg64_ones()
    for p in range(y_ref.shape[1] // SEG_W):
        sl = slice(p * SEG_W, (p + 1) * SEG_W)
        y = y_ref[:, sl]
        dv = y - _seg64_sum(y, ones) * (1.0 / HEAD)
        var = _seg64_sum(dv * dv, ones) * (1.0 / HEAD)
        yn = dv * lax.rsqrt(var + GN_EPS) * lng_ref[:, sl] + lnb_ref[:, sl]
        bonus = _seg64_sum(r_ref[:, sl] * k_ref[:, sl] * rk_ref[:, sl], ones) * v_ref[:, sl]
        o_scr[:, sl] = ((yn + bonus) * g_ref[:, sl]).astype(o_scr.dtype)
    xn = x_ref[...] + _dot(o_scr[...], w_ref[...])
    xo_ref[...] = xn
    ho_ref[...] = _rms(xn, gn_ref[...]).astype(ho_ref.dtype)


def rwkv_post(y, r, k2, v, g, ln_g, ln_b, r_k, w_out, x2, gn, tm=256):
    t, d = x2.shape
    row = lambda i: (i, 0)
    fixed = lambda i: (0, 0)
    tile = pl.BlockSpec((tm, d), row)
    vecs = pl.BlockSpec((1, d), fixed)
    return pl.pallas_call(
        _rwkv_post_kernel,
        out_shape=(jax.ShapeDtypeStruct((t, d), F32), jax.ShapeDtypeStruct((t, d), BF16)),
        grid=(t // tm,),
        in_specs=[tile] * 5 + [vecs] * 3 + [pl.BlockSpec((d, d), fixed), tile, vecs],
        out_specs=(tile, tile),
        scratch_shapes=[pltpu.VMEM((tm, d), BF16)],
        compiler_params=_cp("parallel"),
    )(y, r, k2, v, g, ln_g.reshape(1, d), ln_b.reshape(1, d), r_k.reshape(1, d), w_out.astype(BF16), x2,
      gn.reshape(1, d))


S5_L = 16


def _split(a):
    hi = a.astype(BF16)
    return hi, (a - hi.astype(F32)).astype(BF16)


def _s5_carry_kernel(er_ref, ei_ref, lr_ref, li_ref, pr_ref, pi_ref):
    lr = lr_ref[...]
    li = li_ref[...]

    def body(n, st):
        sr, si = st
        pr_ref[n] = sr
        pi_ref[n] = si
        return (lr * sr - li * si + er_ref[n], lr * si + li * sr + ei_ref[n])
    zero = jnp.zeros(lr.shape, F32)
    lax.fori_loop(0, er_ref.shape[0], body, (zero, zero))


S5_TG = 8
S5_NB = 256


def _dot2(a, b):
    a_hi, a_lo = _split(a)
    return _dot(a_hi, b) + _dot(a_lo, b)


def _s5_rows(x_ref):
    return jnp.concatenate([x_ref[:, t, :] for t in range(x_ref.shape[1])], axis=1)


def _s5t_local_kernel(x_ref, gr_ref, gi_ref, xr_ref, xi_ref):
    u = _s5_rows(x_ref)
    xr_ref[...] = _dot2(u, gr_ref[0])
    xi_ref[...] = _dot2(u, gi_ref[0])


def _s5t_out_kernel(x_ref, pr_ref, pi_ref, k_ref, er_ref, ei_ref, y_ref):
    y = _dot2(_s5_rows(x_ref), k_ref[0]) + _dot2(pr_ref[...], er_ref[0]) + _dot2(pi_ref[...], ei_ref[0])
    for t in range(y_ref.shape[1]):
        y_ref[:, t, :] = y[:, t * LANES:(t + 1) * LANES]


def s5_ssm(h3, a_re, a_im, log_step, b_re, b_im, c_re, c_im):
    bsz, seq, d = h3.shape
    ng, ns = a_re.shape
    gc = d // ng
    L, tg = S5_L, S5_TG
    nc = seq // L
    n = bsz * nc
    nj = ng // tg
    assert tg * gc == LANES and 2 * ns == LANES and n % S5_NB == 0 and L % 2 == 0
    step = jnp.exp(log_step.astype(F32))[:, None]
    lam = lax.complex(a_re.astype(F32), a_im.astype(F32))
    lam_bar = jnp.exp(lam * step)
    b_bar = ((lam_bar - 1.0) / lam)[..., None] * lax.complex(b_re.astype(F32), b_im.astype(F32))
    cc = lax.complex(c_re.astype(F32), c_im.astype(F32))
    pw = jnp.exp((lam * step)[:, None, :] * jnp.arange(L + 1, dtype=F32)[None, :, None])
    lag = jnp.arange(L)[None, :] - jnp.arange(L)[:, None]
    kfull = jnp.einsum('gcp,gstp,gpe->gsetc', cc, pw[:, jnp.clip(lag, 0, L)], b_bar)
    kmat = jnp.where((lag >= 0)[None, :, None, :, None], jnp.real(kfull), 0.0)
    gfull = jnp.einsum('gsp,gpe->gsep', pw[:, L - 1 - jnp.arange(L)], b_bar)
    efull = jnp.einsum('gcp,gtp->gptc', cc, pw[:, 1:])
    lam_l = pw[:, L]
    eye = jnp.eye(tg, dtype=F32)
    w = L * LANES
    k6 = kmat.reshape(nj, tg, L, gc, L, gc).transpose(0, 2, 1, 3, 4, 5)
    k2 = (k6[:, :, :, :, :, None, :] * eye[None, None, :, None, None, :, None]).reshape(nj, w, w).astype(BF16)
    g5 = gfull.reshape(nj, tg, L, gc, ns).transpose(0, 2, 1, 3, 4)
    spread_g = lambda m: (m[:, :, :, :, None, :] * eye[None, None, :, None, :, None]).reshape(nj, w, tg * ns)
    g2r = spread_g(jnp.real(g5)).astype(BF16)
    g2i = spread_g(jnp.imag(g5)).astype(BF16)
    e5 = efull.reshape(nj, tg, ns, L, gc)
    spread_e = lambda m: (m[:, :, :, :, None, :] * eye[None, :, None, None, :, None]).reshape(nj, tg * ns, w)
    e2r = spread_e(jnp.real(e5)).astype(BF16)
    e2i = spread_e(-jnp.imag(e5)).astype(BF16)
    nq = ng // 2
    lr = jnp.real(lam_l).reshape(nq, LANES)
    li = jnp.imag(lam_l).reshape(nq, LANES)

    x4 = h3.reshape(n, L, d)
    nb = S5_NB
    sw = tg * ns
    x_spec = pl.BlockSpec((nb, L, LANES), lambda j, i: (i, 0, j))
    xr, xi = pl.pallas_call(
        _s5t_local_kernel,
        out_shape=(jax.ShapeDtypeStruct((n, nj * sw), F32),) * 2,
        grid=(nj, n // nb),
        in_specs=[x_spec, pl.BlockSpec((1, w, sw), lambda j, i: (j, 0, 0)),
                  pl.BlockSpec((1, w, sw), lambda j, i: (j, 0, 0))],
        out_specs=(pl.BlockSpec((nb, sw), lambda j, i: (i, j)),) * 2,
        compiler_params=_cp("parallel", "parallel"),
    )(x4, g2r, g2i)
    st_spec = pl.BlockSpec((nc, 8, LANES), lambda b, q: (b, q, 0))
    lam_spec = pl.BlockSpec((8, LANES), lambda b, q: (q, 0))
    pr, pi = pl.pallas_call(
        _s5_carry_kernel,
        out_shape=(jax.ShapeDtypeStruct((n, nq, LANES), F32),) * 2,
        grid=(bsz, nq // 8),
        in_specs=[st_spec, st_spec, lam_spec, lam_spec],
        out_specs=(st_spec, st_spec),
        compiler_params=_cp("parallel", "parallel"),
    )(xr.reshape(n, nq, LANES), xi.reshape(n, nq, LANES), lr, li)
    hw = w // 2
    p_spec = pl.BlockSpec((nb, sw), lambda j, hh, i: (i, j))
    e_spec = pl.BlockSpec((1, sw, hw), lambda j, hh, i: (j, 0, hh))
    y = pl.pallas_call(
        _s5t_out_kernel,
        out_shape=jax.ShapeDtypeStruct((n, L, d), F32),
        grid=(nj, 2, n // nb),
        in_specs=[pl.BlockSpec((nb, L, LANES), lambda j, hh, i: (i, 0, j)), p_spec, p_spec,
                  pl.BlockSpec((1, w, hw), lambda j, hh, i: (j, 0, hh)), e_spec, e_spec],
        out_specs=pl.BlockSpec((nb, L // 2, LANES), lambda j, hh, i: (i, hh, j)),
        compiler_params=_cp("parallel", "parallel", "parallel"),
    )(x4, pr.reshape(n, nq * LANES), pi.reshape(n, nq * LANES), k2, e2r, e2i)
    return y.reshape(bsz * seq, d)


def _s5_glu_kernel(ys_ref, h_ref, d_ref, w_ref, b_ref, x_ref, gn_ref, xo_ref, ho_ref):
    y = ys_ref[...] + d_ref[...] * h_ref[...]
    gelu = 0.5 * y * (1.0 + jnp.tanh(math.sqrt(2.0 / math.pi) * (y + 0.044715 * (y * y * y))))
    z = _dot(gelu.astype(BF16), w_ref[...]) + b_ref[...]
    dm = x_ref.shape[1]
    xn = x_ref[...] + z[:, :dm] * jax.nn.sigmoid(z[:, dm:])
    xo_ref[...] = xn
    ho_ref[...] = _rms(xn, gn_ref[...]).astype(ho_ref.dtype)


def s5_glu(ys, h, d_skip, w_glu, b_glu, x2, gn, tm=256):
    t, d = x2.shape
    row = lambda i: (i, 0)
    fixed = lambda i: (0, 0)
    tile = pl.BlockSpec((tm, d), row)
    return pl.pallas_call(
        _s5_glu_kernel,
        out_shape=(jax.ShapeDtypeStruct((t, d), F32), jax.ShapeDtypeStruct((t, d), BF16)),
        grid=(t // tm,),
        in_specs=[tile, tile, pl.BlockSpec((1, d), fixed), pl.BlockSpec((d, 2 * d), fixed),
                  pl.BlockSpec((1, 2 * d), fixed), tile, pl.BlockSpec((1, d), fixed)],
        out_specs=(tile, tile),
        compiler_params=_cp("parallel"),
    )(ys, h, d_skip.reshape(1, d), w_glu.astype(BF16), b_glu.reshape(1, 2 * d), x2, gn.reshape(1, d))


def kernel(x, norm_mix, norm_ffn, dsa_w_in, dsa_q_norm, dsa_k_norm, dsa_kidx_norm, dsa_w_out, rwkv_mu, rwkv_w_rkv, rwkv_w0, rwkv_w1, rwkv_w2, rwkv_a0, rwkv_a1, rwkv_a2, rwkv_g1, rwkv_g2, rwkv_k_k, rwkv_k_a, rwkv_r_k, rwkv_ln_g, rwkv_ln_b, rwkv_w_out, s5_a_re, s5_a_im, s5_log_step, s5_b_re, s5_b_im, s5_c_re, s5_c_im, s5_d, s5_w_glu, s5_b_glu, ffn_w_up, ffn_conv_w, ffn_conv_b, ffn_w_down):
    bsz, seq, d = x.shape
    depth = norm_mix.shape[0]
    x2 = x.reshape(bsz * seq, d)
    for i in range(depth):
        kind, j = i % 3, i // 3
        if kind == 0:
            h = rmsnorm(x2, norm_mix[i], BF16)
            o = dsa_mixer(h, bsz, seq, dsa_w_in[j], dsa_q_norm[j], dsa_k_norm[j], dsa_kidx_norm[j])
            x2, h = mm_res_norm(o, dsa_w_out[j].astype(BF16), x2, norm_ffn[i], BF16)
        elif kind == 1:
            r, lw, k2, v, kk, b, g = rwkv_proj(x2, seq, norm_mix[i], rwkv_mu[j], rwkv_w_rkv[j], rwkv_w0[j],
                                               rwkv_w1[j], rwkv_w2[j], rwkv_a0[j], rwkv_a1[j], rwkv_a2[j],
                                               rwkv_g1[j], rwkv_g2[j], rwkv_k_k[j], rwkv_k_a[j])
            y = rwkv_scan(r, lw, k2, v, kk, b, bsz, seq)
            x2, h = rwkv_post(y, r, k2, v, g, rwkv_ln_g[j], rwkv_ln_b[j], rwkv_r_k[j].reshape(d), rwkv_w_out[j],
                              x2, norm_ffn[i])
        else:
            hf = rmsnorm(x2, norm_mix[i], F32)
            ys = s5_ssm(hf.reshape(bsz, seq, d), s5_a_re[j], s5_a_im[j], s5_log_step[j], s5_b_re[j], s5_b_im[j],
                        s5_c_re[j], s5_c_im[j])
            x2, h = s5_glu(ys, hf, s5_d[j], s5_w_glu[j], s5_b_glu[j], x2, norm_ffn[i])
        act = ffn_up(h, ffn_w_up[i].astype(BF16), ffn_conv_w[i], ffn_conv_b[i], seq)
        x2, _ = mm_res_norm(act, ffn_w_down[i].astype(BF16), x2, None, None)
    return x2.reshape(bsz, seq, d)
```

```python
import functools
import math

import jax
import jax.numpy as jnp
from jax import lax
from jax.experimental import pallas as pl
from jax.experimental.pallas import tpu as pltpu

F32 = jnp.float32
BF16 = jnp.bfloat16
I32 = jnp.int32

EPS = 1e-6
NEG_INF = -1e30
LANES = 128
HEAD = 64
CHUNK = 64
TOPK_MAX = 256
GN_EPS = 64e-5
INT_MIN = -(2 ** 31)

VMEM_LIMIT = 56 * 1024 * 1024


def _cp(*sem):
    return pltpu.CompilerParams(dimension_semantics=sem, vmem_limit_bytes=VMEM_LIMIT)


def _nt(a, b):
    return lax.dot_general(a, b, (((1,), (1,)), ((), ())), preferred_element_type=F32)


def _tn(a, b):
    return lax.dot_general(a, b, (((0,), (0,)), ((), ())), preferred_element_type=F32)


def _dot(a, b):
    return jnp.dot(a, b, preferred_element_type=F32)


def _rms(x, g):
    ms = jnp.mean(x * x, axis=-1, keepdims=True)
    return x * lax.rsqrt(ms + EPS) * g


SEG_W = 2 * LANES


def _seg64_ones():
    r = lax.broadcasted_iota(I32, (SEG_W, SEG_W), 0) // HEAD
    c = lax.broadcasted_iota(I32, (SEG_W, SEG_W), 1) // HEAD
    return jnp.where(r == c, 1.0, 0.0).astype(BF16)


def _seg64_sum(x, ones):
    hi = x.astype(BF16)
    lo = (x - hi.astype(F32)).astype(BF16)
    return _dot(hi, ones) + _dot(lo, ones)


def _norm_kernel(x_ref, g_ref, o_ref):
    o_ref[...] = _rms(x_ref[...], g_ref[...]).astype(o_ref.dtype)


def rmsnorm(x2, g, out_dtype, tm=1024):
    t, d = x2.shape
    return pl.pallas_call(
        _norm_kernel,
        out_shape=jax.ShapeDtypeStruct((t, d), out_dtype),
        grid=(t // tm,),
        in_specs=[pl.BlockSpec((tm, d), lambda i: (i, 0)),
                  pl.BlockSpec((1, d), lambda i: (0, 0))],
        out_specs=pl.BlockSpec((tm, d), lambda i: (i, 0)),
        compiler_params=_cp("parallel"),
    )(x2, g.reshape(1, d))


def _mm_res_norm_kernel(a_ref, w_ref, x_ref, g_ref, xo_ref, ho_ref):
    xn = x_ref[...] + _dot(a_ref[...], w_ref[...])
    xo_ref[...] = xn
    ho_ref[...] = _rms(xn, g_ref[...]).astype(ho_ref.dtype)


def _mm_res_kernel(a_ref, w_ref, x_ref, xo_ref):
    xo_ref[...] = x_ref[...] + _dot(a_ref[...], w_ref[...])


def mm_res_norm(a, w, x2, g, h_dtype, tm=512):
    t, k = a.shape
    d = w.shape[1]
    row = lambda i: (i, 0)
    fixed = lambda i: (0, 0)
    in_specs = [pl.BlockSpec((tm, k), row), pl.BlockSpec((k, d), fixed), pl.BlockSpec((tm, d), row)]
    if g is None:
        return pl.pallas_call(
            _mm_res_kernel,
            out_shape=jax.ShapeDtypeStruct((t, d), F32),
            grid=(t // tm,), in_specs=in_specs, out_specs=pl.BlockSpec((tm, d), row),
            compiler_params=_cp("parallel"),
        )(a, w, x2), None
    return pl.pallas_call(
        _mm_res_norm_kernel,
        out_shape=(jax.ShapeDtypeStruct((t, d), F32), jax.ShapeDtypeStruct((t, d), h_dtype)),
        grid=(t // tm,),
        in_specs=in_specs + [pl.BlockSpec((1, d), fixed)],
        out_specs=(pl.BlockSpec((tm, d), row), pl.BlockSpec((tm, d), row)),
        compiler_params=_cp("parallel"),
    )(a, w, x2, g.reshape(1, d))


PREV_ROWS = 16


FFN_TN = 256


def _ffn_up_kernel(h_ref, hp_ref, w_ref, cw_ref, cb_ref, o_ref, *, tiles_per_seq):
    i = pl.program_id(0)
    h = h_ref[...]
    hp = hp_ref[...]
    f = o_ref.shape[1]
    seq_start = (i % tiles_per_seq) == 0
    head = PREV_ROWS
    rows = lax.broadcasted_iota(I32, (head, FFN_TN), 0)

    def project(c):
        out = []
        for base in (0, f):
            cols = slice(base + c * FFN_TN, base + (c + 1) * FFN_TN)
            w = w_ref[:, cols]
            out.append((_dot(h, w), jnp.where(seq_start, 0.0, _dot(hp, w)), cols))
        return out

    def conv(u, up, cols):
        cw = cw_ref[:, cols]
        cb = cb_ref[:, cols]
        mix = lambda u2, u1, u0: cw[0:1, :] * u2 + cw[1:2, :] * u1 + cw[2:3, :] * u0 + cb
        full = mix(pltpu.roll(u, 2, 0), pltpu.roll(u, 1, 0), u)
        uh = u[0:head]
        last = up[PREV_ROWS - 1:PREV_ROWS, :]
        u1 = jnp.where(rows == 0, last, pltpu.roll(uh, 1, 0))
        u2 = jnp.where(rows == 0, up[PREV_ROWS - 2:PREV_ROWS - 1, :],
                       jnp.where(rows == 1, last, pltpu.roll(uh, 2, 0)))
        return full, mix(u2, u1, uh)

    n_chunks = f // FFN_TN
    nxt = project(0)
    for c in range(n_chunks):
        (ug, upg, cg), (uv, upv, cv) = nxt
        if c + 1 < n_chunks:
            nxt = project(c + 1)
        gate, gate_h = conv(ug, upg, cg)
        val, val_h = conv(uv, upv, cv)
        o_ref[:, cg] = (gate * jax.nn.sigmoid(gate) * val).astype(o_ref.dtype)
        o_ref[0:head, cg] = (gate_h * jax.nn.sigmoid(gate_h) * val_h).astype(o_ref.dtype)


def ffn_up(h, w_up, conv_w, conv_b, seq, tm=512):
    t, d = h.shape
    f = w_up.shape[1] // 2
    assert f % FFN_TN == 0 and seq % tm == 0
    cw = jnp.zeros((8, 2 * f), F32).at[:conv_w.shape[0]].set(conv_w)
    cb = conv_b.reshape(1, 2 * f)
    fixed = lambda i: (0, 0)
    return pl.pallas_call(
        functools.partial(_ffn_up_kernel, tiles_per_seq=seq // tm),
        out_shape=jax.ShapeDtypeStruct((t, f), BF16),
        grid=(t // tm,),
        in_specs=[pl.BlockSpec((tm, d), lambda i: (i, 0)),
                  pl.BlockSpec((PREV_ROWS, d), lambda i: (jnp.maximum(i * (tm // PREV_ROWS) - 1, 0), 0)),
                  pl.BlockSpec((d, 2 * f), fixed),
                  pl.BlockSpec((8, 2 * f), fixed),
                  pl.BlockSpec((1, 2 * f), fixed)],
        out_specs=pl.BlockSpec((tm, f), lambda i: (i, 0)),
        compiler_params=_cp("parallel"),
    )(h, h, w_up, cw, cb)


PAIRS = 8
IDX_PAIRS = 4
TQ = 128
TK = 512


def _dsa_proj_kernel(h_ref, wq_ref, wk_ref, wvt_ref, wi_ref, gq_ref, gk_ref, gki_ref,
                     q_ref, k_ref, vt_ref, qi_ref, ki_ref, wi_out_ref):
    h = h_ref[...]
    tm = h.shape[0]

    ones = _seg64_ones()

    def head_norm(y, g_ref, o_ref):
        for p2 in range(PAIRS // 2):
            sl = slice(p2 * SEG_W, (p2 + 1) * SEG_W)
            yp = y[:, sl]
            ms = _seg64_sum(yp * yp, ones) * (1.0 / HEAD)
            yn = (yp * lax.rsqrt(ms + EPS) * g_ref[:, sl]).astype(o_ref.dtype)
            o_ref[0, 2 * p2] = yn[:, :LANES]
            o_ref[0, 2 * p2 + 1] = yn[:, LANES:]

    head_norm(_dot(h, wq_ref[...]), gq_ref, q_ref)
    head_norm(_dot(h, wk_ref[...]), gk_ref, k_ref)
    vt = _nt(wvt_ref[...], h)
    vt_ref[0, :, :LANES, :] = vt.reshape(PAIRS, LANES, tm).astype(vt_ref.dtype)
    vt_ref[0, :, LANES:, :] = jnp.ones((PAIRS, vt_ref.shape[2] - LANES, tm), vt_ref.dtype)
    idx = _dot(h, wi_ref[...])
    for p in range(IDX_PAIRS):
        qi_ref[0, p] = (idx[:, p * LANES:(p + 1) * LANES] * (HEAD ** -0.5)).astype(qi_ref.dtype)
    kw = idx[:, IDX_PAIRS * LANES:]
    ms = _seg64_sum(kw * kw, ones)[:, :LANES] * (1.0 / HEAD)
    ki_ref[0] = (kw[:, :LANES] * lax.rsqrt(ms + EPS) * gki_ref[...]).astype(ki_ref.dtype)
    wi_out_ref[0] = idx[:, (IDX_PAIRS + 1) * LANES:] * (2 * IDX_PAIRS) ** -0.5


def dsa_proj(h, bsz, seq, w_in, q_gain, k_gain, kidx_gain, tm=256):
    t, d = h.shape
    n_idx = 2 * IDX_PAIRS
    wq = w_in[:, :d].astype(BF16)
    wk = w_in[:, d:2 * d].astype(BF16)
    wvt = w_in[:, 2 * d:3 * d].T.astype(BF16)
    o = 3 * d
    w_qi = w_in[:, o:o + n_idx * HEAD]
    w_ki = w_in[:, o + n_idx * HEAD:o + n_idx * HEAD + HEAD]
    w_wi = w_in[:, o + n_idx * HEAD + HEAD:]
    w_wi = jnp.pad(w_wi, ((0, 0), (0, LANES - n_idx)))
    wi = jnp.concatenate([w_qi, w_ki, w_ki, w_wi], axis=1).astype(BF16)
    gq = (jnp.tile(q_gain, d // HEAD) * (HEAD ** -0.5 * math.log2(math.e))).reshape(1, d)
    gk = jnp.tile(k_gain, d // HEAD).reshape(1, d)
    gki = jnp.tile(kidx_gain, 2).reshape(1, LANES)
    nb = seq // tm
    fixed = lambda b, i: (0, 0)
    return pl.pallas_call(
        _dsa_proj_kernel,
        out_shape=(jax.ShapeDtypeStruct((bsz, PAIRS, seq, LANES), BF16),
                   jax.ShapeDtypeStruct((bsz, PAIRS, seq, LANES), BF16),
                   jax.ShapeDtypeStruct((bsz, PAIRS, VT_ROWS, seq), BF16),
                   jax.ShapeDtypeStruct((bsz, IDX_PAIRS, seq, LANES), BF16),
                   jax.ShapeDtypeStruct((bsz, seq, LANES), BF16),
                   jax.ShapeDtypeStruct((bsz, seq, LANES), F32)),
        grid=(bsz, nb),
        in_specs=[pl.BlockSpec((tm, d), lambda b, i: (b * nb + i, 0)),
                  pl.BlockSpec((d, d), fixed), pl.BlockSpec((d, d), fixed), pl.BlockSpec((d, d), fixed),
                  pl.BlockSpec((d, wi.shape[1]), fixed),
                  pl.BlockSpec((1, d), fixed), pl.BlockSpec((1, d), fixed), pl.BlockSpec((1, LANES), fixed)],
        out_specs=(pl.BlockSpec((1, PAIRS, tm, LANES), lambda b, i: (b, 0, i, 0)),
                   pl.BlockSpec((1, PAIRS, tm, LANES), lambda b, i: (b, 0, i, 0)),
                   pl.BlockSpec((1, PAIRS, VT_ROWS, tm), lambda b, i: (b, 0, 0, i)),
                   pl.BlockSpec((1, IDX_PAIRS, tm, LANES), lambda b, i: (b, 0, i, 0)),
                   pl.BlockSpec((1, tm, LANES), lambda b, i: (b, i, 0)),
                   pl.BlockSpec((1, tm, LANES), lambda b, i: (b, i, 0))),
        compiler_params=_cp("parallel", "parallel"),
    )(h, wq, wk, wvt, wi, gq, gk, gki)


def _pair_split(x):
    lane = lax.broadcasted_iota(I32, x.shape, 1)
    zero = jnp.zeros_like(x)
    return jnp.concatenate([jnp.where(lane < HEAD, x, zero), jnp.where(lane >= HEAD, x, zero)], axis=0)


EXP_MASK = 0x7F800000


def _dsa_select_kernel(qi_ref, ki_ref, wi_ref, mask_ref, keys_ref, *, seq, topk):
    j = pl.program_id(1)
    lane = lax.broadcasted_iota(I32, (1, TQ), 1)
    limq = j * TQ + (lane // CHUNK + 1) * CHUNK
    nkt = (j * TQ + TQ + TK - 1) // TK
    wit = wi_ref[0].T
    w_pairs = [_pair_split(qi_ref[0, p]) for p in range(IDX_PAIRS)]
    kiota = lax.broadcasted_iota(I32, (TK, TQ), 0)

    def score_body(kt, carry):
        off = pl.multiple_of(kt * TK, TK)
        kit = ki_ref[0, pl.ds(off, TK), :]
        lgs = [_nt(kit, w_pairs[p]) for p in range(IDX_PAIRS)]
        s = jnp.zeros((TK, TQ), F32)
        for p in range(IDX_PAIRS):
            s = s + jnp.maximum(lgs[p][:, :TQ], 0.0) * wit[2 * p:2 * p + 1, :]
            s = s + jnp.maximum(lgs[p][:, TQ:], 0.0) * wit[2 * p + 1:2 * p + 2, :]
        s = jnp.where(off + kiota < limq, s, NEG_INF)
        bits = pltpu.bitcast(s, I32)
        bits = jnp.where((bits & EXP_MASK) == 0, 0, bits)
        keys_ref[pl.ds(off, TK), :] = jnp.where(bits < 0, bits ^ 0x7FFFFFFF, bits)
        return carry

    lax.fori_loop(0, nkt, score_body, 0)

    def count(pred):
        def body(i, acc):
            off = pl.multiple_of(i * TK, TK)
            hit = jnp.where(pred(keys_ref[pl.ds(off, TK), :], off), 1, 0)
            return acc + hit.reshape(TK // 8, 8, TQ).sum(axis=0)
        acc = lax.fori_loop(0, nkt, body, jnp.zeros((8, TQ), I32))
        return acc.sum(axis=0, keepdims=True)

    def search(_):
        def bit_body(t, tu):
            cand_u = tu | jnp.left_shift(jnp.int32(1), 31 - t)
            cand_s = cand_u ^ INT_MIN
            c = count(lambda kb, off: kb >= cand_s)
            return jnp.where(c >= topk, cand_u, tu)
        tu = lax.fori_loop(0, 32, bit_body, jnp.zeros((1, TQ), I32))
        return tu ^ INT_MIN

    ts = lax.cond(j * TQ + TQ > topk, search, lambda _: jnp.full((1, TQ), INT_MIN, I32), 0)

    n_ge = count(lambda kb, off: kb >= ts)

    @pl.when(jnp.max(n_ge) > topk)
    def _break_ties():
        r = topk - count(lambda kb, off: kb > ts)

        def bit_body(t, p):
            cand = p | jnp.left_shift(jnp.int32(1), (seq.bit_length() - 1) - t)
            c = count(lambda kb, off: (kb == ts) & (off + kiota < cand))
            return jnp.where(c < r, cand, p)
        p_last = lax.fori_loop(0, seq.bit_length(), bit_body, jnp.zeros((1, TQ), I32))

        def demote(kt, carry):
            off = pl.multiple_of(kt * TK, TK)
            kb = keys_ref[pl.ds(off, TK), :]
            keys_ref[pl.ds(off, TK), :] = jnp.where((kb == ts) & (off + kiota > p_last), ts - 1, kb)
            return carry
        lax.fori_loop(0, nkt, demote, 0)

    def out_body(kt, carry):
        off = pl.multiple_of(kt * TK, TK)

        @pl.when(kt < nkt)
        def _():
            kb = keys_ref[pl.ds(off, TK), :]
            sel = (kb >= ts) & (off + kiota < limq)
            mask_ref[0, 0, pl.ds(off, TK), :] = jnp.where(sel, 1, 0).astype(mask_ref.dtype)

        @pl.when(kt >= nkt)
        def _():
            mask_ref[0, 0, pl.ds(off, TK), :] = jnp.zeros((TK, TQ), mask_ref.dtype)
        return carry

    lax.fori_loop(0, seq // TK, out_body, 0)


def dsa_select(qi, ki, wi, topk):
    bsz, _, seq, _ = qi.shape
    nq = seq // TQ
    return pl.pallas_call(
        functools.partial(_dsa_select_kernel, seq=seq, topk=topk),
        out_shape=jax.ShapeDtypeStruct((bsz, nq, seq, TQ), jnp.int8),
        grid=(bsz, nq),
        in_specs=[pl.BlockSpec((1, IDX_PAIRS, TQ, LANES), lambda b, j: (b, 0, j, 0)),
                  pl.BlockSpec((1, seq, LANES), lambda b, j: (b, 0, 0)),
                  pl.BlockSpec((1, TQ, LANES), lambda b, j: (b, j, 0))],
        out_specs=pl.BlockSpec((1, 1, seq, TQ), lambda b, j: (b, j, 0, 0)),
        scratch_shapes=[pltpu.VMEM((seq, TQ), I32)],
        compiler_params=_cp("parallel", "parallel"),
    )(qi, ki, wi)


VT_ROWS = LANES + 16
S_CHUNK = 64
S_AHEAD = 3
S_SLOTS = S_AHEAD + 1


def _dsa_attn_kernel(jmap_ref, ktmap_ref, q_ref, k_ref, vt_ref, mask_ref, o_ref,
                     qm_ref, m_ref, l_ref, acc_ref, bias_ref, s_ref, p_ref):
    step = pl.program_id(1)
    j = jmap_ref[step]
    kt = ktmap_ref[step]
    last = (j * TQ + TQ - 1) // TK

    @pl.when(kt == 0)
    def _init():
        for p in range(PAIRS):
            qm_ref[p] = _pair_split(q_ref[0, p].astype(F32)).T.astype(BF16)
        m_ref[...] = jnp.full(m_ref.shape, -jnp.inf, F32)
        l_ref[...] = jnp.zeros(l_ref.shape, F32)
        acc_ref[...] = jnp.zeros(acc_ref.shape, F32)

    bias = (mask_ref[0, 0].astype(F32) - 1.0) * 1e30
    bias_ref[:, :TQ] = bias
    bias_ref[:, TQ:] = bias

    def scores(p):
        m_tile = None
        for hf in range(2):
            rows = slice(hf * (TK // 2), (hf + 1) * (TK // 2))
            s = _dot(k_ref[0, p, rows, :], qm_ref[p]) + bias_ref[rows, :]
            s_ref[p % S_SLOTS, rows, :] = s
            m_half = jnp.max(s, axis=0, keepdims=True)
            m_tile = m_half if m_tile is None else jnp.maximum(m_tile, m_half)
        return m_tile

    m_tiles = [scores(p) for p in range(S_AHEAD)]
    for p in range(PAIRS):
        slot = p % S_SLOTS
        m_old = m_ref[p]
        m_new = jnp.maximum(m_old, m_tiles[p])
        alpha = jnp.exp2(m_old - m_new)
        m_ref[p] = m_new
        if p + S_AHEAD < PAIRS:
            m_tiles.append(scores(p + S_AHEAD))
        for c in range(TK // S_CHUNK):
            rows = slice(c * S_CHUNK, (c + 1) * S_CHUNK)
            p_ref[slot, rows, :] = jnp.exp2(s_ref[slot, rows, :] - m_new).astype(BF16)
        pv = _dot(vt_ref[0, p], p_ref[slot])
        acc_ref[p] = acc_ref[p] * alpha + pv[:LANES]
        l_ref[p] = l_ref[p] * alpha + pv[LANES:LANES + 1]

    @pl.when(kt == last)
    def _finish():
        for p in range(PAIRS):
            a = acc_ref[p] / l_ref[p]
            ot = jnp.concatenate([a[0:HEAD, 0:TQ], a[HEAD:2 * HEAD, TQ:2 * TQ]], axis=0)
            o_ref[0, :, p * LANES:(p + 1) * LANES] = ot.T.astype(o_ref.dtype)


def dsa_attn(q, k, vt, mask):
    bsz, _, seq, _ = q.shape
    nq = seq // TQ
    visits = [(j, t) for j in range(nq) for t in range((j * TQ + TQ - 1) // TK + 1)]
    jmap = jnp.asarray([jt[0] for jt in visits], I32)
    ktmap = jnp.asarray([jt[1] for jt in visits], I32)
    grid_spec = pltpu.PrefetchScalarGridSpec(
        num_scalar_prefetch=2,
        grid=(bsz, len(visits)),
        in_specs=[pl.BlockSpec((1, PAIRS, TQ, LANES), lambda b, s, jm, km: (b, 0, jm[s], 0)),
                  pl.BlockSpec((1, PAIRS, TK, LANES), lambda b, s, jm, km: (b, 0, km[s], 0)),
                  pl.BlockSpec((1, PAIRS, VT_ROWS, TK), lambda b, s, jm, km: (b, 0, 0, km[s])),
                  pl.BlockSpec((1, 1, TK, TQ), lambda b, s, jm, km: (b, jm[s], km[s], 0))],
        out_specs=pl.BlockSpec((1, TQ, PAIRS * LANES), lambda b, s, jm, km: (b, jm[s], 0)),
        scratch_shapes=[pltpu.VMEM((PAIRS, LANES, 2 * TQ), BF16),
                        pltpu.VMEM((PAIRS, 1, 2 * TQ), F32),
                        pltpu.VMEM((PAIRS, 1, 2 * TQ), F32),
                        pltpu.VMEM((PAIRS, LANES, 2 * TQ), F32),
                        pltpu.VMEM((TK, 2 * TQ), F32),
                        pltpu.VMEM((S_SLOTS, TK, 2 * TQ), F32),
                        pltpu.VMEM((S_SLOTS, TK, 2 * TQ), BF16)])
    return pl.pallas_call(
        _dsa_attn_kernel,
        out_shape=jax.ShapeDtypeStruct((bsz, seq, PAIRS * LANES), BF16),
        grid_spec=grid_spec,
        compiler_params=_cp("parallel", "arbitrary"),
    )(jmap, ktmap, q, k, vt, mask)


def dsa_mixer(h, bsz, seq, w_in, q_gain, k_gain, kidx_gain):
    assert seq % TK == 0 and TK % TQ == 0
    topk = min(TOPK_MAX, seq // 4)
    q, k, vt, qi, ki, wi = dsa_proj(h, bsz, seq, w_in, q_gain, k_gain, kidx_gain)
    mask = dsa_select(qi, ki, wi, topk)
    return dsa_attn(q, k, vt, mask).reshape(bsz * seq, PAIRS * LANES)


RW_PREV = 8
RW_C = 64
RW_CH = 8


def _rwkv_proj_kernel(x_ref, xp_ref, gn_ref, mu_ref, wr_ref, wk_ref, wv_ref, w1_ref, w2_ref, a1_ref, a2_ref,
                      g1_ref, g2_ref, w0_ref, a0_ref, kk_ref, ka_ref,
                      r_out, lw_out, k_out, v_out, kk_out, b_out, g_out, *, tiles_per_seq):
    i = pl.program_id(0)
    gn = gn_ref[...]
    h = _rms(x_ref[...], gn)
    hp = _rms(xp_ref[...], gn)
    hp_last = jnp.where((i % tiles_per_seq) == 0, 0.0, hp[RW_PREV - 1:RW_PREV, :])
    rows = lax.broadcasted_iota(I32, h.shape, 0)
    dh = jnp.where(rows == 0, hp_last, pltpu.roll(h, 1, 0)) - h
    mu = mu_ref[...]
    xs = lambda n: (h + dh * mu[n:n + 1, :]).astype(BF16)
    r = _dot(xs(0), wr_ref[...])
    k = _dot(xs(1), wk_ref[...])
    v_out[...] = _dot(xs(2), wv_ref[...])
    wl = w0_ref[...] + _dot(jnp.tanh(_dot(xs(3), w1_ref[...])).astype(BF16), w2_ref[...])
    z = -wl
    w_log = -(jnp.maximum(z, 0.0) + jnp.log(1.0 + jnp.exp(-jnp.abs(z)))) - 0.5
    lw_out[...] = -jnp.exp(w_log)
    a = jax.nn.sigmoid(a0_ref[...] + _dot(_dot(xs(4), a1_ref[...]).astype(BF16), a2_ref[...]))
    g_out[...] = _dot(jax.nn.sigmoid(_dot(xs(5), g1_ref[...])).astype(BF16), g2_ref[...])
    r_out[...] = r
    k_out[...] = k * (1.0 + (a - 1.0) * ka_ref[...])
    kk = k * kk_ref[...]
    ones = _seg64_ones()
    for p in range(kk.shape[1] // SEG_W):
        sl = slice(p * SEG_W, (p + 1) * SEG_W)
        kp = kk[:, sl]
        kn = kp / jnp.maximum(jnp.sqrt(_seg64_sum(kp * kp, ones)), 1e-12)
        kk_out[:, sl] = kn
        b_out[:, sl] = kn * a[:, sl]


def rwkv_proj(x2, seq, gn, mu, w_rkv, w0, w1, w2, a0, a1, a2, g1, g2, k_k, k_a, tm=256):
    t, d = x2.shape
    pad_c = lambda w: jnp.pad(w, ((0, 0), (0, LANES - w.shape[1]))).astype(BF16)
    pad_r = lambda w: jnp.pad(w, ((0, LANES - w.shape[0]), (0, 0))).astype(BF16)
    row = lambda i: (i, 0)
    fixed = lambda i: (0, 0)
    vec = lambda a: a.reshape(1, d)
    full = lambda a: pl.BlockSpec(a.shape, fixed)
    args = [x2, x2, vec(gn), jnp.pad(mu, ((0, 2), (0, 0))),
            w_rkv[0].astype(BF16), w_rkv[1].astype(BF16), w_rkv[2].astype(BF16),
            pad_c(w1), pad_r(w2), pad_c(a1), pad_r(a2), g1.astype(BF16), g2.astype(BF16),
            vec(w0), vec(a0), vec(k_k), vec(k_a)]
    in_specs = [pl.BlockSpec((tm, d), row),
                pl.BlockSpec((RW_PREV, d), lambda i: (jnp.maximum(i * (tm // RW_PREV) - 1, 0), 0))]
    in_specs += [full(a) for a in args[2:]]
    return pl.pallas_call(
        functools.partial(_rwkv_proj_kernel, tiles_per_seq=seq // tm),
        out_shape=tuple(jax.ShapeDtypeStruct((t, d), F32) for _ in range(7)),
        grid=(t // tm,),
        in_specs=in_specs,
        out_specs=tuple(pl.BlockSpec((tm, d), row) for _ in range(7)),
        compiler_params=_cp("parallel"),
    )(*args)


def _bdot(a, b):
    return _dot(a.astype(BF16), b.astype(BF16))


def _rwkv_chunk_terms(r, lw, k2, v, kk, b, tick):
    c = RW_C
    cat = jnp.concatenate
    each = lambda f, *ls: [f(*a) for a in zip(*ls)]
    row = lax.broadcasted_iota(I32, (2 * c, 2 * c), 0)
    col = lax.broadcasted_iota(I32, (2 * c, 2 * c), 1)
    r_c = lax.broadcasted_iota(I32, (c, c), 0)
    c_c = lax.broadcasted_iota(I32, (c, c), 1)
    tri = jnp.where(c_c <= r_c, 1.0, 0.0).astype(BF16)

    def cum(lw_):
        hi = lw_.astype(BF16)
        rem = lw_ - hi.astype(F32)
        mid = rem.astype(BF16)
        lo = (rem - mid.astype(F32)).astype(BF16)
        return _dot(tri, hi) + _dot(tri, mid) + _dot(tri, lo)
    cs = each(cum, lw)
    e_pos = each(jnp.exp, cs)
    e_neg = each(lambda x: jnp.exp(-x), cs)
    mul = lambda x, y: x * y
    rt = each(mul, r, e_pos)
    kt = each(mul, k2, e_neg)
    bt = each(mul, b, e_neg)
    kkt = each(lambda x, s, l: x * jnp.exp(s - l), kk, cs, lw)
    g_end = each(lambda e: e[c - 1:c, :], e_pos)
    khat = each(mul, kt, g_end)
    bhat = each(mul, bt, g_end)
    lane = lax.broadcasted_iota(I32, (c, LANES), 1)
    t_row = lax.broadcasted_iota(I32, (c, LANES), 0)
    m0 = lane < HEAD
    h0 = lambda x: jnp.where(m0, x, 0.0)
    h1 = lambda x: jnp.where(m0, 0.0, x)
    g0 = each(lambda kq, rr, bb, kk_: _nt(cat([h0(kq), h0(rr)]).astype(BF16), cat([bb, kk_]).astype(BF16)),
              kkt, rt, bt, kt)
    tick()
    g1 = each(lambda kq, rr, bb, kk_: _nt(cat([h1(rr), h1(kq)]).astype(BF16), cat([kk_, bb]).astype(BF16)),
              kkt, rt, bt, kt)
    top, left = row < c, col < c
    m_ab0, m_ab1 = top & left & (col < row), ~top & ~left & (col < row)
    m_ak0, m_ak1 = top & ~left & (col - c < row), ~top & left & (col < row - c)
    a_b = each(lambda x0, x1: jnp.where(m_ab0, x0, 0.0) + jnp.where(m_ab1, x1, 0.0), g0, g1)
    a_k = each(lambda x0, x1: jnp.where(m_ak0, x0, 0.0) + jnp.where(m_ak1, x1, 0.0), g0, g1)
    eye = jnp.where(row == col, 1.0, 0.0)
    x = each(lambda a: eye - a, a_b)
    pw = each(lambda a: _bdot(a, a), a_b)
    tick()
    for it in range(5):
        x = each(lambda xx, pp: xx + _bdot(xx, pp), x, pw)
        tick()
        if it < 4:
            pw = each(lambda pp: _bdot(pp, pp), pw)
    v0, v1 = each(h0, v), each(h1, v)
    akv = each(lambda a, va, vb: _bdot(a, cat([vb, va])), a_k, v0, v1)
    tick()
    wu = each(lambda xx, kq, av: _bdot(xx, cat([cat([h0(kq), h1(kq)]), av], axis=1)), x, kkt, akv)
    incl = jnp.where(m0, lane, lane - HEAD) <= t_row
    zeros = jnp.zeros((c, LANES), F32)
    rhs0 = each(lambda w_, va: cat([-w_[0:c], cat([zeros, va], axis=1)]), wu, v0)
    rhs1 = each(lambda w_, vb: cat([cat([zeros, vb], axis=1), -w_[c:2 * c]]), wu, v1)
    o0 = each(lambda g, rh: _bdot(jnp.where(incl, g[c:2 * c], 0.0), rh), g0, rhs0)
    o1 = each(lambda g, rh: _bdot(jnp.where(incl, g[0:c], 0.0), rh), g1, rhs1)
    rq = each(lambda rr, a0, a1: rr + a0[:, :LANES] + a1[:, :LANES], rt, o0, o1)
    yin = each(lambda a0, a1: a0[:, LANES:] + a1[:, LANES:], o0, o1)
    pd = each(lambda bh, kh, ra, rb: _tn(cat([h0(bh), h0(kh), h1(kh), h1(bh)]).astype(BF16),
                                         cat([ra, rb]).astype(BF16)), bhat, khat, rhs0, rhs1)
    phi = each(lambda p_, ge: p_[:, :LANES] + jnp.where(row == col, jnp.broadcast_to(ge, (2 * c, LANES)), 0.0),
               pd, g_end)
    return [(a, b_, c_, p_[:, LANES:]) for a, b_, c_, p_ in zip(rq, yin, phi, pd)]


def _rwkv_scan_kernel(r_ref, lw_ref, k_ref, v_ref, kk_ref, b_ref, y_ref, h_ref, lhs_ref, yin_ref, dh_ref):
    s = pl.program_id(2)
    cur = s % 2
    prev = 1 - cur
    rows = [slice(ci * RW_C, (ci + 1) * RW_C) for ci in range(RW_CH)]

    @pl.when(s == 0)
    def _():
        lhs_ref[prev] = jnp.zeros(lhs_ref.shape[1:], lhs_ref.dtype)
        yin_ref[prev] = jnp.zeros(yin_ref.shape[1:], F32)
        dh_ref[prev] = jnp.zeros(dh_ref.shape[1:], F32)
        h_ref[...] = jnp.zeros(h_ref.shape, F32)

    state = [jnp.where(s <= 1, 0.0, h_ref[...])]
    links = iter(range(RW_CH))

    def recurrence_step():
        ci = next(links, None)
        if ci is None:
            return
        both = _dot(lhs_ref[prev, ci], state[0].astype(BF16))
        y_ref[0, rows[ci], :] = both[0:RW_C] + yin_ref[prev, ci]
        state[0] = both[RW_C:] + dh_ref[prev, ci]

    ld = lambda ref: [ref[0, sl, :] for sl in rows]
    terms = _rwkv_chunk_terms(ld(r_ref), ld(lw_ref), ld(k_ref), ld(v_ref), ld(kk_ref), ld(b_ref), recurrence_step)
    for _ in range(RW_CH):
        recurrence_step()
    h_ref[...] = state[0]
    for ci, (rq, yin, phi, dh) in enumerate(terms):
        lhs_ref[cur, ci] = jnp.concatenate([rq, phi]).astype(BF16)
        yin_ref[cur, ci] = yin
        dh_ref[cur, ci] = dh


def rwkv_scan(r, lw, k2, v, kk, b, bsz, seq):
    d = r.shape[-1]
    rows = RW_C * RW_CH
    nblk = seq // rows
    in_spec = pl.BlockSpec((1, rows, LANES), lambda bb, p, s: (bb, jnp.minimum(s, nblk - 1), p))
    out_spec = pl.BlockSpec((1, rows, LANES), lambda bb, p, s: (bb, jnp.maximum(s - 1, 0), p))
    shp = lambda a: a.reshape(bsz, seq, d)
    return pl.pallas_call(
        _rwkv_scan_kernel,
        out_shape=jax.ShapeDtypeStruct((bsz, seq, d), F32),
        grid=(bsz, d // LANES, nblk + 1),
        in_specs=[in_spec] * 6,
        out_specs=out_spec,
        scratch_shapes=[pltpu.VMEM((LANES, LANES), F32),
                        pltpu.VMEM((2, RW_CH, RW_C + LANES, LANES), BF16),
                        pltpu.VMEM((2, RW_CH, RW_C, LANES), F32),
                        pltpu.VMEM((2, RW_CH, LANES, LANES), F32)],
        compiler_params=_cp("parallel", "parallel", "arbitrary"),
    )(shp(r), shp(lw), shp(k2), shp(v), shp(kk), shp(b)).reshape(bsz * seq, d)


def _rwkv_post_kernel(y_ref, r_ref, k_ref, v_ref, g_ref, lng_ref, lnb_ref, rk_ref, w_ref, x_ref, gn_ref,
                      xo_ref, ho_ref, o_scr):
    ones = _seg64_ones()
    for p in range(y_ref.shape[1] // SEG_W):
        sl = slice(p * SEG_W, (p + 1) * SEG_W)
        y = y_ref[:, sl]
        dv = y - _seg64_sum(y, ones) * (1.0 / HEAD)
        var = _seg64_sum(dv * dv, ones) * (1.0 / HEAD)
        yn = dv * lax.rsqrt(var + GN_EPS) * lng_ref[:, sl] + lnb_ref[:, sl]
        bonus = _seg64_sum(r_ref[:, sl] * k_ref[:, sl] * rk_ref[:, sl], ones) * v_ref[:, sl]
        o_scr[:, sl] = ((yn + bonus) * g_ref[:, sl]).astype(o_scr.dtype)
    xn = x_ref[...] + _dot(o_scr[...], w_ref[...])
    xo_ref[...] = xn
    ho_ref[...] = _rms(xn, gn_ref[...]).astype(ho_ref.dtype)


def rwkv_post(y, r, k2, v, g, ln_g, ln_b, r_k, w_out, x2, gn, tm=256):
    t, d = x2.shape
    row = lambda i: (i, 0)
    fixed = lambda i: (0, 0)
    tile = pl.BlockSpec((tm, d), row)
    vecs = pl.BlockSpec((1, d), fixed)
    return pl.pallas_call(
        _rwkv_post_kernel,
        out_shape=(jax.ShapeDtypeStruct((t, d), F32), jax.ShapeDtypeStruct((t, d), BF16)),
        grid=(t // tm,),
        in_specs=[tile] * 5 + [vecs] * 3 + [pl.BlockSpec((d, d), fixed), tile, vecs],
        out_specs=(tile, tile),
        scratch_shapes=[pltpu.VMEM((tm, d), BF16)],
        compiler_params=_cp("parallel"),
    )(y, r, k2, v, g, ln_g.reshape(1, d), ln_b.reshape(1, d), r_k.reshape(1, d), w_out.astype(BF16), x2,
      gn.reshape(1, d))


S5_L = 16


def _split(a):
    hi = a.astype(BF16)
    return hi, (a - hi.astype(F32)).astype(BF16)


def _s5_carry_kernel(er_ref, ei_ref, lr_ref, li_ref, pr_ref, pi_ref):
    lr = lr_ref[...]
    li = li_ref[...]

    def body(n, st):
        sr, si = st
        pr_ref[n] = sr
        pi_ref[n] = si
        return (lr * sr - li * si + er_ref[n], lr * si + li * sr + ei_ref[n])
    zero = jnp.zeros(lr.shape, F32)
    lax.fori_loop(0, er_ref.shape[0], body, (zero, zero))


S5_TG = 8
S5_NB = 256


def _dot2(a, b):
    a_hi, a_lo = _split(a)
    return _dot(a_hi, b) + _dot(a_lo, b)


def _s5_rows(x_ref):
    return jnp.concatenate([x_ref[:, t, :] for t in range(x_ref.shape[1])], axis=1)


def _s5t_local_kernel(x_ref, gr_ref, gi_ref, xr_ref, xi_ref):
    u = _s5_rows(x_ref)
    xr_ref[...] = _dot2(u, gr_ref[0])
    xi_ref[...] = _dot2(u, gi_ref[0])


def _s5t_out_kernel(x_ref, pr_ref, pi_ref, k_ref, er_ref, ei_ref, y_ref):
    y = _dot2(_s5_rows(x_ref), k_ref[0]) + _dot2(pr_ref[...], er_ref[0]) + _dot2(pi_ref[...], ei_ref[0])
    for t in range(y_ref.shape[1]):
        y_ref[:, t, :] = y[:, t * LANES:(t + 1) * LANES]


def s5_ssm(h3, a_re, a_im, log_step, b_re, b_im, c_re, c_im):
    bsz, seq, d = h3.shape
    ng, ns = a_re.shape
    gc = d // ng
    L, tg = S5_L, S5_TG
    nc = seq // L
    n = bsz * nc
    nj = ng // tg
    assert tg * gc == LANES and 2 * ns == LANES and n % S5_NB == 0 and L % 2 == 0
    step = jnp.exp(log_step.astype(F32))[:, None]
    lam = lax.complex(a_re.astype(F32), a_im.astype(F32))
    lam_bar = jnp.exp(lam * step)
    b_bar = ((lam_bar - 1.0) / lam)[..., None] * lax.complex(b_re.astype(F32), b_im.astype(F32))
    cc = lax.complex(c_re.astype(F32), c_im.astype(F32))
    pw = jnp.exp((lam * step)[:, None, :] * jnp.arange(L + 1, dtype=F32)[None, :, None])
    lag = jnp.arange(L)[None, :] - jnp.arange(L)[:, None]
    kfull = jnp.einsum('gcp,gstp,gpe->gsetc', cc, pw[:, jnp.clip(lag, 0, L)], b_bar)
    kmat = jnp.where((lag >= 0)[None, :, None, :, None], jnp.real(kfull), 0.0)
    gfull = jnp.einsum('gsp,gpe->gsep', pw[:, L - 1 - jnp.arange(L)], b_bar)
    efull = jnp.einsum('gcp,gtp->gptc', cc, pw[:, 1:])
    lam_l = pw[:, L]
    w = L * LANES
    hp = lax.Precision.HIGHEST
    sw_ = tg * ns
    r_w = jnp.arange(w)
    r_s = jnp.arange(sw_)
    grp_w = (r_w // gc) % tg
    grp_s = r_s // ns
    ex_w = ((r_w[None, :] // LANES == jnp.arange(L * gc)[:, None] // gc)
            & (r_w[None, :] % gc == jnp.arange(L * gc)[:, None] % gc)).astype(F32)
    ex_s = (r_s[None, :] % ns == jnp.arange(ns)[:, None]).astype(F32)
    k_a = kmat.reshape(nj, tg, L, gc, L * gc).transpose(0, 2, 1, 3, 4).reshape(nj, w, L * gc)
    k2 = (jnp.matmul(k_a, ex_w, precision=hp) * (grp_w[:, None] == grp_w[None, :])).astype(BF16)
    g_a = gfull.reshape(nj, tg, L, gc, ns).transpose(0, 2, 1, 3, 4).reshape(nj, w, ns)
    same_ws = grp_w[:, None] == grp_s[None, :]
    g2r = (jnp.matmul(jnp.real(g_a), ex_s, precision=hp) * same_ws).astype(BF16)
    g2i = (jnp.matmul(jnp.imag(g_a), ex_s, precision=hp) * same_ws).astype(BF16)
    e_a = efull.reshape(nj, sw_, L * gc)
    same_sw = grp_s[:, None] == grp_w[None, :]
    e2r = (jnp.matmul(jnp.real(e_a), ex_w, precision=hp) * same_sw).astype(BF16)
    e2i = (jnp.matmul(-jnp.imag(e_a), ex_w, precision=hp) * same_sw).astype(BF16)
    nq = ng // 2
    lr = jnp.real(lam_l).reshape(nq, LANES)
    li = jnp.imag(lam_l).reshape(nq, LANES)

    x4 = h3.reshape(n, L, d)
    nb = S5_NB
    sw = tg * ns
    x_spec = pl.BlockSpec((nb, L, LANES), lambda j, i: (i, 0, j))
    xr, xi = pl.pallas_call(
        _s5t_local_kernel,
        out_shape=(jax.ShapeDtypeStruct((n, nj * sw), F32),) * 2,
        grid=(nj, n // nb),
        in_specs=[x_spec, pl.BlockSpec((1, w, sw), lambda j, i: (j, 0, 0)),
                  pl.BlockSpec((1, w, sw), lambda j, i: (j, 0, 0))],
        out_specs=(pl.BlockSpec((nb, sw), lambda j, i: (i, j)),) * 2,
        compiler_params=_cp("parallel", "parallel"),
    )(x4, g2r, g2i)
    st_spec = pl.BlockSpec((nc, 8, LANES), lambda b, q: (b, q, 0))
    lam_spec = pl.BlockSpec((8, LANES), lambda b, q: (q, 0))
    pr, pi = pl.pallas_call(
        _s5_carry_kernel,
        out_shape=(jax.ShapeDtypeStruct((n, nq, LANES), F32),) * 2,
        grid=(bsz, nq // 8),
        in_specs=[st_spec, st_spec, lam_spec, lam_spec],
        out_specs=(st_spec, st_spec),
        compiler_params=_cp("parallel", "parallel"),
    )(xr.reshape(n, nq, LANES), xi.reshape(n, nq, LANES), lr, li)
    hw = w // 2
    p_spec = pl.BlockSpec((nb, sw), lambda j, hh, i: (i, j))
    e_spec = pl.BlockSpec((1, sw, hw), lambda j, hh, i: (j, 0, hh))
    y = pl.pallas_call(
        _s5t_out_kernel,
        out_shape=jax.ShapeDtypeStruct((n, L, d), F32),
        grid=(nj, 2, n // nb),
        in_specs=[pl.BlockSpec((nb, L, LANES), lambda j, hh, i: (i, 0, j)), p_spec, p_spec,
                  pl.BlockSpec((1, w, hw), lambda j, hh, i: (j, 0, hh)), e_spec, e_spec],
        out_specs=pl.BlockSpec((nb, L // 2, LANES), lambda j, hh, i: (i, hh, j)),
        compiler_params=_cp("parallel", "parallel", "parallel"),
    )(x4, pr.reshape(n, nq * LANES), pi.reshape(n, nq * LANES), k2, e2r, e2i)
    return y.reshape(bsz * seq, d)


def _s5_glu_kernel(ys_ref, h_ref, d_ref, w_ref, b_ref, x_ref, gn_ref, xo_ref, ho_ref):
    y = ys_ref[...] + d_ref[...] * h_ref[...]
    gelu = 0.5 * y * (1.0 + jnp.tanh(math.sqrt(2.0 / math.pi) * (y + 0.044715 * (y * y * y))))
    z = _dot(gelu.astype(BF16), w_ref[...]) + b_ref[...]
    dm = x_ref.shape[1]
    xn = x_ref[...] + z[:, :dm] * jax.nn.sigmoid(z[:, dm:])
    xo_ref[...] = xn
    ho_ref[...] = _rms(xn, gn_ref[...]).astype(ho_ref.dtype)


def s5_glu(ys, h, d_skip, w_glu, b_glu, x2, gn, tm=256):
    t, d = x2.shape
    row = lambda i: (i, 0)
    fixed = lambda i: (0, 0)
    tile = pl.BlockSpec((tm, d), row)
    return pl.pallas_call(
        _s5_glu_kernel,
        out_shape=(jax.ShapeDtypeStruct((t, d), F32), jax.ShapeDtypeStruct((t, d), BF16)),
        grid=(t // tm,),
        in_specs=[tile, tile, pl.BlockSpec((1, d), fixed), pl.BlockSpec((d, 2 * d), fixed),
                  pl.BlockSpec((1, 2 * d), fixed), tile, pl.BlockSpec((1, d), fixed)],
        out_specs=(tile, tile),
        compiler_params=_cp("parallel"),
    )(ys, h, d_skip.reshape(1, d), w_glu.astype(BF16), b_glu.reshape(1, 2 * d), x2, gn.reshape(1, d))


def kernel(x, norm_mix, norm_ffn, dsa_w_in, dsa_q_norm, dsa_k_norm, dsa_kidx_norm, dsa_w_out, rwkv_mu, rwkv_w_rkv, rwkv_w0, rwkv_w1, rwkv_w2, rwkv_a0, rwkv_a1, rwkv_a2, rwkv_g1, rwkv_g2, rwkv_k_k, rwkv_k_a, rwkv_r_k, rwkv_ln_g, rwkv_ln_b, rwkv_w_out, s5_a_re, s5_a_im, s5_log_step, s5_b_re, s5_b_im, s5_c_re, s5_c_im, s5_d, s5_w_glu, s5_b_glu, ffn_w_up, ffn_conv_w, ffn_conv_b, ffn_w_down):
    bsz, seq, d = x.shape
    depth = norm_mix.shape[0]
    x2 = x.reshape(bsz * seq, d)
    for i in range(depth):
        kind, j = i % 3, i // 3
        if kind == 0:
            h = rmsnorm(x2, norm_mix[i], BF16)
            o = dsa_mixer(h, bsz, seq, dsa_w_in[j], dsa_q_norm[j], dsa_k_norm[j], dsa_kidx_norm[j])
            x2, h = mm_res_norm(o, dsa_w_out[j].astype(BF16), x2, norm_ffn[i], BF16)
        elif kind == 1:
            r, lw, k2, v, kk, b, g = rwkv_proj(x2, seq, norm_mix[i], rwkv_mu[j], rwkv_w_rkv[j], rwkv_w0[j],
                                               rwkv_w1[j], rwkv_w2[j], rwkv_a0[j], rwkv_a1[j], rwkv_a2[j],
                                               rwkv_g1[j], rwkv_g2[j], rwkv_k_k[j], rwkv_k_a[j])
            y = rwkv_scan(r, lw, k2, v, kk, b, bsz, seq)
            x2, h = rwkv_post(y, r, k2, v, g, rwkv_ln_g[j], rwkv_ln_b[j], rwkv_r_k[j].reshape(d), rwkv_w_out[j],
                              x2, norm_ffn[i])
        else:
            hf = rmsnorm(x2, norm_mix[i], F32)
            ys = s5_ssm(hf.reshape(bsz, seq, d), s5_a_re[j], s5_a_im[j], s5_log_step[j], s5_b_re[j], s5_b_im[j],
                        s5_c_re[j], s5_c_im[j])
            x2, h = s5_glu(ys, hf, s5_d[j], s5_w_glu[j], s5_b_glu[j], x2, norm_ffn[i])
        act = ffn_up(h, ffn_w_up[i].astype(BF16), ffn_conv_w[i], ffn_conv_b[i], seq)
        x2, _ = mm_res_norm(act, ffn_w_down[i].astype(BF16), x2, None, None)
    return x2.reshape(bsz, seq, d)
```
